```python
import math
import jax, jax.numpy as jnp
from jax import lax
import numpy as np

D_MODEL = 1024
BATCH = 4
SEQ = 4096
DEPTH = 1

HYENA_WIDTH = 1024
SHORT_CONV = 3
FILTER_EMB = 33
FILTER_BANDS = (FILTER_EMB - 1) // 2
FILTER_HIDDEN = 64
FILTER_OUT_SCALE = 0.04
DECAY_TARGET = 1e-2
DECAY_FAST = 0.3
DECAY_SLOW = 1.5
SGU_WIDTH = 1024
SGU_HEADS = 8
SGU_HEAD_DIM = SGU_WIDTH // SGU_HEADS
SGU_CHUNK = 128
FFN_HIDDEN = 2816
FFN_CONV = 3
EPS = 1e-6
IN_WIDTH = 3 * HYENA_WIDTH + 2 * SGU_WIDTH + 2 * D_MODEL

kernel_name = "hyena_sgu_gated_hybrid_encoder"


def rms_norm(x, g):
    xf = x.astype(jnp.float32)
    y = xf * lax.rsqrt(jnp.mean(xf * xf, axis=-1, keepdims=True) + EPS)
    return (y * g.astype(jnp.float32)).astype(x.dtype)


def depthwise_conv_centred(x, w, b):
    y = lax.conv_general_dilated(
        x, w[:, None, :].astype(x.dtype), window_strides=(1,), padding="SAME",
        dimension_numbers=("NWC", "WIO", "NWC"), feature_group_count=x.shape[-1])
    return y + b.astype(x.dtype)


def implicit_filters(L, w1, b1, w2, b2, w3, b3, freq, w4, decay):
    f32 = jnp.float32
    t = jnp.linspace(0.0, 1.0, L, dtype=f32)[:, None]
    bands = jnp.linspace(1e-4, FILTER_BANDS - 1, FILTER_BANDS, dtype=f32)[None, :]
    phase = (2.0 * math.pi / L) * jnp.arange(L, dtype=f32)[:, None] * bands
    z = jnp.concatenate([t, jnp.cos(phase), -jnp.sin(phase)], axis=-1)
    a = freq.astype(f32)
    h = jnp.sin(a * (z @ w1.astype(f32) + b1.astype(f32)))
    h = jnp.sin(a * (h @ w2.astype(f32) + b2.astype(f32)))
    h = jnp.sin(a * (h @ w3.astype(f32) + b3.astype(f32)))
    h = h @ w4.astype(f32)
    return h * jnp.exp(-t * jnp.abs(decay.astype(f32)))


def bidirectional_long_conv(u, h_fwd, h_bwd, skip):
    L, C = u.shape[1], u.shape[2]
    k = jnp.concatenate([h_fwd, jnp.zeros((1, C), jnp.float32), h_bwd[:0:-1]], axis=0)
    uf = u.astype(jnp.float32)
    spec = jnp.fft.rfft(uf, n=2 * L, axis=1) * jnp.fft.rfft(k, axis=0)[None]
    y = jnp.fft.irfft(spec, n=2 * L, axis=1)[:, :L]
    return (y + uf * skip.astype(jnp.float32)).astype(u.dtype)


def hyena_branch(p, conv_w, conv_b, fw1, fb1, fw2, fb2, fw3, fb3, ffreq, fw4, decay, skip):
    p = depthwise_conv_centred(p, conv_w, conv_b)
    x0, x1, v = jnp.split(p, 3, axis=-1)
    filt = implicit_filters(p.shape[1], fw1, fb1, fw2, fb2, fw3, fb3, ffreq, fw4, decay)
    return x0 * bidirectional_long_conv(x1 * v, filt[:, :HYENA_WIDTH], filt[:, HYENA_WIDTH:], skip)


def spatial_gating_branch(p, norm_g, w_s, b_s):
    B, L, _ = p.shape
    u, v = jnp.split(jax.nn.gelu(p, approximate=False), 2, axis=-1)
    v = rms_norm(v, norm_g).reshape(B, L // SGU_CHUNK, SGU_CHUNK, SGU_HEADS, SGU_HEAD_DIM)
    s = jnp.einsum("gpq,bnqgd->bnpgd", w_s.astype(v.dtype), v) + b_s.T.astype(v.dtype)[:, :, None]
    return u * s.reshape(B, L, SGU_WIDTH)


def hybrid_layer(x, norm1_g, w_in, hy_conv_w, hy_conv_b, filt_w1, filt_b1, filt_w2, filt_b2,
                 filt_w3, filt_b3, filt_freq, filt_w4, hy_decay, hy_skip, sgu_norm_g, sgu_w,
                 sgu_b, w_proj_hyena, w_proj_sgu, w_out, norm2_g, w_up, ffn_conv_w, ffn_conv_b,
                 w_down):
    h = rms_norm(x, norm1_g)
    proj = h @ w_in
    p_hy, p_sgu, p_gate = jnp.split(
        proj, [3 * HYENA_WIDTH, 3 * HYENA_WIDTH + 2 * SGU_WIDTH], axis=-1)
    y_a = hyena_branch(p_hy, hy_conv_w, hy_conv_b, filt_w1, filt_b1, filt_w2, filt_b2,
                       filt_w3, filt_b3, filt_freq, filt_w4, hy_decay, hy_skip)
    y_b = spatial_gating_branch(p_sgu, sgu_norm_g, sgu_w, sgu_b)
    g_a, g_b = jnp.split(jax.nn.sigmoid(p_gate), 2, axis=-1)
    merged = g_a * (y_a @ w_proj_hyena) + g_b * (y_b @ w_proj_sgu)
    x = x + merged @ w_out
    h = rms_norm(x, norm2_g)
    a, gate = jnp.split(h @ w_up, 2, axis=-1)
    a = depthwise_conv_centred(a, ffn_conv_w, ffn_conv_b)
    return x + (jax.nn.silu(a) * gate) @ w_down


def setup_inputs(seed: int = 0) -> dict:
    key = jax.random.key(seed)
    ks = iter(jax.random.split(key, 32))
    f32 = jnp.float32

    def nrm(shape, scale):
        return scale * jax.random.normal(next(ks), shape, f32)

    Dp = DEPTH
    decay_lo = math.log(DECAY_TARGET) / DECAY_SLOW
    decay_hi = math.log(DECAY_TARGET) / DECAY_FAST
    decay_base = jnp.tile(jnp.linspace(decay_lo, decay_hi, HYENA_WIDTH, dtype=f32), 2)
    return {
        "x": nrm((BATCH, SEQ, D_MODEL), 1.0),
        "norm1_g": 1.0 + nrm((Dp, D_MODEL), 0.1),
        "w_in": nrm((Dp, D_MODEL, IN_WIDTH), D_MODEL ** -0.5),
        "hy_conv_w": nrm((Dp, SHORT_CONV, 3 * HYENA_WIDTH), SHORT_CONV ** -0.5),
        "hy_conv_b": nrm((Dp, 3 * HYENA_WIDTH), 0.01),
        "filt_w1": nrm((Dp, FILTER_EMB, FILTER_HIDDEN), FILTER_EMB ** -0.5),
        "filt_b1": nrm((Dp, FILTER_HIDDEN), 0.2),
        "filt_w2": nrm((Dp, FILTER_HIDDEN, FILTER_HIDDEN), FILTER_HIDDEN ** -0.5),
        "filt_b2": nrm((Dp, FILTER_HIDDEN), 0.2),
        "filt_w3": nrm((Dp, FILTER_HIDDEN, FILTER_HIDDEN), FILTER_HIDDEN ** -0.5),
        "filt_b3": nrm((Dp, FILTER_HIDDEN), 0.2),
        "filt_freq": 1.0 + nrm((Dp, FILTER_HIDDEN), 0.1),
        "filt_w4": nrm((Dp, FILTER_HIDDEN, 2 * HYENA_WIDTH), FILTER_OUT_SCALE * FILTER_HIDDEN ** -0.5),
        "hy_decay": decay_base * (1.0 + nrm((Dp, 2 * HYENA_WIDTH), 0.05)),
        "hy_skip": nrm((Dp, HYENA_WIDTH), 0.5),
        "sgu_norm_g": 1.0 + nrm((Dp, SGU_WIDTH), 0.1),
        "sgu_w": nrm((Dp, SGU_HEADS, SGU_CHUNK, SGU_CHUNK), SGU_CHUNK ** -0.5),
        "sgu_b": 1.0 + nrm((Dp, SGU_HEADS, SGU_CHUNK), 0.1),
        "w_proj_hyena": nrm((Dp, HYENA_WIDTH, D_MODEL), HYENA_WIDTH ** -0.5),
        "w_proj_sgu": nrm((Dp, SGU_WIDTH, D_MODEL), SGU_WIDTH ** -0.5),
        "w_out": nrm((Dp, D_MODEL, D_MODEL), D_MODEL ** -0.5),
        "norm2_g": 1.0 + nrm((Dp, D_MODEL), 0.1),
        "w_up": nrm((Dp, D_MODEL, 2 * FFN_HIDDEN), D_MODEL ** -0.5),
        "ffn_conv_w": nrm((Dp, FFN_CONV, FFN_HIDDEN), FFN_CONV ** -0.5),
        "ffn_conv_b": nrm((Dp, FFN_HIDDEN), 0.01),
        "w_down": nrm((Dp, FFN_HIDDEN, D_MODEL), FFN_HIDDEN ** -0.5),
        "final_g": 1.0 + nrm((D_MODEL,), 0.1),
    }


def reference(x, norm1_g, w_in, hy_conv_w, hy_conv_b, filt_w1, filt_b1, filt_w2, filt_b2,
              filt_w3, filt_b3, filt_freq, filt_w4, hy_decay, hy_skip, sgu_norm_g, sgu_w, sgu_b,
              w_proj_hyena, w_proj_sgu, w_out, norm2_g, w_up, ffn_conv_w, ffn_conv_b, w_down,
              final_g):
    for i in range(DEPTH):
        x = hybrid_layer(
            x, norm1_g[i], w_in[i], hy_conv_w[i], hy_conv_b[i], filt_w1[i], filt_b1[i],
            filt_w2[i], filt_b2[i], filt_w3[i], filt_b3[i], filt_freq[i], filt_w4[i],
            hy_decay[i], hy_skip[i], sgu_norm_g[i], sgu_w[i], sgu_b[i], w_proj_hyena[i],
            w_proj_sgu[i], w_out[i], norm2_g[i], w_up[i], ffn_conv_w[i], ffn_conv_b[i],
            w_down[i])
    return rms_norm(x, final_g)
```

```python
import functools
import math

import numpy as np
import jax
import jax.numpy as jnp
from jax import lax
from jax.experimental import pallas as pl
from jax.experimental.pallas import tpu as pltpu

EPS = 1e-6
F32 = jnp.float32
BF16 = jnp.bfloat16

LANES = 128
BF16_SUBLANE_TILE = 16
VMEM_LIMIT_BYTES = 60 * 1024 * 1024

FFT_INNER = 128
CONV_ROWS = 256
HYENA_CH = 256


def _round_up(n, m):
    return -(-n // m) * m


def _cparams(n_axes):
    return pltpu.CompilerParams(
        dimension_semantics=("arbitrary",) * n_axes, vmem_limit_bytes=VMEM_LIMIT_BYTES)


def _const_spec(shape):
    nd = len(shape)
    return pl.BlockSpec(shape, lambda *_: (0,) * nd, pipeline_mode=pl.Buffered(1))


def _inproj_kernel(x_ref, g_ref, w_ref, o_ref, hn_ref, *, n_raw, n_gelu):
    j = pl.program_id(1)

    @pl.when(j == 0)
    def _():
        xv = x_ref[...]
        ms = jnp.mean(xv * xv, axis=-1, keepdims=True)
        hn_ref[...] = (xv * lax.rsqrt(ms + EPS) * g_ref[...]).astype(BF16)

    acc = jnp.dot(hn_ref[...], w_ref[...], preferred_element_type=F32)

    @pl.when(j < n_raw)
    def _():
        o_ref[...] = acc.astype(o_ref.dtype)

    @pl.when(jnp.logical_and(j >= n_raw, j < n_raw + n_gelu))
    def _():
        o_ref[...] = (0.5 * acc * (1.0 + lax.erf(acc * math.sqrt(0.5)))).astype(o_ref.dtype)

    @pl.when(j >= n_raw + n_gelu)
    def _():
        o_ref[...] = (1.0 / (1.0 + jnp.exp(-acc))).astype(o_ref.dtype)


def _in_projection(x2d, g, w_bf16, n_raw_cols, n_gelu_cols, tm, tn):
    t, d = x2d.shape
    n = w_bf16.shape[1]
    assert t % tm == 0 and n % tn == 0 and n_raw_cols % tn == 0 and n_gelu_cols % tn == 0
    kern = functools.partial(_inproj_kernel, n_raw=n_raw_cols // tn, n_gelu=n_gelu_cols // tn)
    return pl.pallas_call(
        kern,
        grid=(t // tm, n // tn),
        in_specs=[
            pl.BlockSpec((tm, d), lambda i, j: (i, 0)),
            pl.BlockSpec((1, d), lambda i, j: (0, 0)),
            pl.BlockSpec((d, tn), lambda i, j: (0, j)),
        ],
        out_specs=pl.BlockSpec((tm, tn), lambda i, j: (i, j)),
        out_shape=jax.ShapeDtypeStruct((t, n), BF16),
        scratch_shapes=[pltpu.VMEM((tm, d), BF16)],
        compiler_params=_cparams(2),
        name="in_projection",
    )(x2d, g, w_bf16)


def _filter_kernel(w1t_ref, w1c_ref, w1s_ref, b1_ref, w2_ref, b2_ref, w3_ref, b3_ref, fr_ref,
                   w4_ref, dec_ref, o_ref, *, seq_len, bands):
    tl = o_ref.shape[0]
    hi = lax.Precision.HIGHEST
    r0 = pl.program_id(0) * tl
    pos = (r0 + lax.broadcasted_iota(jnp.int32, (tl, 1), 0)).astype(F32)
    t = pos / float(seq_len - 1)
    band_step = (bands - 1 - 1e-4) / (bands - 1)
    band = 1e-4 + band_step * lax.broadcasted_iota(jnp.int32, (1, bands), 1).astype(F32)
    phase = (2.0 * math.pi / seq_len) * pos * band
    a = fr_ref[...]
    z1 = (t * w1t_ref[...]
          + jnp.dot(jnp.cos(phase), w1c_ref[...], preferred_element_type=F32, precision=hi)
          - jnp.dot(jnp.sin(phase), w1s_ref[...], preferred_element_type=F32, precision=hi))
    h = jnp.sin(a * (z1 + b1_ref[...]))
    h = jnp.sin(a * (jnp.dot(h, w2_ref[...], preferred_element_type=F32, precision=hi) + b2_ref[...]))
    h = jnp.sin(a * (jnp.dot(h, w3_ref[...], preferred_element_type=F32, precision=hi) + b3_ref[...]))
    f = jnp.dot(h, w4_ref[...], preferred_element_type=F32, precision=hi)
    o_ref[...] = f * jnp.exp(-t * jnp.abs(dec_ref[...]))


def _implicit_filters(seq_len, w1, b1, w2, b2, w3, b3, freq, w4, decay, tl=512):
    bands = (w1.shape[0] - 1) // 2
    hid = w1.shape[1]
    n_out = w4.shape[1]
    assert seq_len % tl == 0
    row = lambda v: v.reshape(1, -1)
    args = (w1[0:1], w1[1:1 + bands], w1[1 + bands:], row(b1), w2, row(b2), w3, row(b3), row(freq),
            w4, row(decay))
    full = lambda a: pl.BlockSpec(a.shape, lambda i: (0, 0))
    kern = functools.partial(_filter_kernel, seq_len=seq_len, bands=bands)
    del hid
    return pl.pallas_call(
        kern,
        grid=(seq_len // tl,),
        in_specs=[full(a) for a in args],
        out_specs=pl.BlockSpec((tl, n_out), lambda i: (i, 0)),
        out_shape=jax.ShapeDtypeStruct((seq_len, n_out), F32),
        compiler_params=_cparams(1),
        name="implicit_filter",
    )(*args)


class _FftPlan:
    def __init__(self, seq_len):
        n2 = FFT_INNER
        assert seq_len % n2 == 0
        self.seq_len = seq_len
        self.n = 2 * seq_len
        self.n2 = n2
        self.n1 = self.n // n2
        self.nb = seq_len // n2
        self.h1 = self.n1 // 2 + 1
        self.pa = _round_up(2 * self.h1, 8)
        self.hp = _round_up(self.h1, 8)
        assert (2 * self.hp) % BF16_SUBLANE_TILE == 0 and self.nb % BF16_SUBLANE_TILE == 0
        n1, nb, h1, n = self.n1, self.nb, self.h1, self.n
        k1 = np.arange(h1)[:, None]
        ph = 2.0 * np.pi * k1 * np.arange(nb)[None, :] / n1
        w1 = np.zeros((self.pa, nb))
        w1[:h1] = np.cos(ph)
        w1[h1:2 * h1] = -np.sin(ph)
        k2 = np.arange(n2)[:, None]
        m2 = np.arange(n2)[None, :]
        w3 = np.zeros((h1, 2 * n2, 2 * n2))
        for k in range(h1):
            th = 2.0 * np.pi * (((k + n1 * k2) * m2) % n) / n
            er, ei = np.cos(th), -np.sin(th)
            w3[k] = np.block([[er, -ei], [ei, er]])
        ck = np.full(h1, 2.0)
        ck[0] = 1.0
        ck[-1] = 1.0
        phi = 2.0 * np.pi * np.arange(nb)[:, None] * np.arange(h1)[None, :] / n1
        mi = np.zeros((nb, 2 * self.hp))
        mi[:, :h1] = ck * np.cos(phi) / n
        mi[:, self.hp:self.hp + h1] = -ck * np.sin(phi) / n
        self.w1 = jnp.asarray(w1, F32).astype(BF16)
        self.w3 = jnp.asarray(w3, F32).astype(BF16)
        self.w3t = jnp.asarray(np.transpose(w3, (0, 2, 1)), F32).astype(BF16)
        self.mi = jnp.asarray(mi, F32).astype(BF16)


def _slab_load(ref, start, size, stride):
    return jnp.concatenate(
        [ref[s, pl.ds(start, size, stride=stride), :] for s in range(ref.shape[0])], axis=1)


def _slab_store(ref, start, size, stride, val):
    for s in range(ref.shape[0]):
        ref[s, pl.ds(start, size, stride=stride), :] = val[:, s * LANES:(s + 1) * LANES]


def _slab_store_block(ref, start, size, val):
    for s in range(ref.shape[0]):
        ref[s, pl.ds(start, size), :] = val[:, s * LANES:(s + 1) * LANES]


def _fft_stage1(plan, u_ref, a_ref, w1_ref):
    w1 = w1_ref[...]

    def body(n2, carry):
        blk = _slab_load(u_ref, n2, plan.nb, plan.n2).astype(BF16)
        a = jnp.dot(w1, blk, preferred_element_type=F32)
        _slab_store_block(a_ref, pl.multiple_of(n2 * plan.pa, 8), plan.pa, a)
        return carry

    lax.fori_loop(0, plan.n2, body, 0, unroll=8)


def _fft_stage2(plan, a_ref, w3_ref, k1):
    re = _slab_load(a_ref, k1, plan.n2, plan.pa)
    im = _slab_load(a_ref, plan.h1 + k1, plan.n2, plan.pa)
    rhs = jnp.concatenate([re, im], axis=0).astype(BF16)
    return jnp.dot(w3_ref[k1], rhs, preferred_element_type=F32)


def _spectrum_kernel(plan, ff_ref, fb_ref, w1_ref, w3_ref, o_ref, u_ref, af_ref, ab_ref):
    n2 = plan.n2
    nslab = u_ref.shape[0]
    for src, dst in ((ff_ref, af_ref), (fb_ref, ab_ref)):
        for s in range(nslab):
            u_ref[s] = src[:, s * LANES:(s + 1) * LANES]
        _fft_stage1(plan, u_ref, dst, w1_ref)
    hb0 = fb_ref[0:1, :]

    def body(k1, carry):
        xf = _fft_stage2(plan, af_ref, w3_ref, k1)
        xb = _fft_stage2(plan, ab_ref, w3_ref, k1)
        o_ref[k1, 0:n2, :] = (xf[:n2] + xb[:n2] - hb0).astype(o_ref.dtype)
        o_ref[k1, n2:2 * n2, :] = (xf[n2:] - xb[n2:]).astype(o_ref.dtype)
        return carry

    lax.fori_loop(0, plan.h1, body, 0)


def _filter_spectrum(plan, filt, ct):
    seq_len, c2 = filt.shape
    c = c2 // 2
    assert c % ct == 0 and ct % LANES == 0
    nct = c // ct
    nslab = ct // LANES
    n2, h1, pa = plan.n2, plan.h1, plan.pa
    return pl.pallas_call(
        functools.partial(_spectrum_kernel, plan),
        grid=(nct,),
        in_specs=[
            pl.BlockSpec((seq_len, ct), lambda j: (0, j)),
            pl.BlockSpec((seq_len, ct), lambda j: (0, nct + j)),
            _const_spec(plan.w1.shape),
            _const_spec(plan.w3.shape),
        ],
        out_specs=pl.BlockSpec((h1, 2 * n2, ct), lambda j: (0, 0, j)),
        out_shape=jax.ShapeDtypeStruct((h1, 2 * n2, c), BF16),
        scratch_shapes=[
            pltpu.VMEM((nslab, seq_len, LANES), F32),
            pltpu.VMEM((nslab, n2 * pa, LANES), F32),
            pltpu.VMEM((nslab, n2 * pa, LANES), F32),
        ],
        compiler_params=_cparams(1),
        name="filter_spectrum",
    )(filt, filt, plan.w1, plan.w3)


def _short_conv_rows(p_ref, w_ref, b_ref, r0, rows, seq_len):
    halo = BF16_SUBLANE_TILE
    main = p_ref[pl.ds(r0, rows), :].astype(F32)
    prev_start = pl.multiple_of(jnp.maximum(r0 - halo, 0), halo)
    next_start = pl.multiple_of(jnp.minimum(r0 + rows, seq_len - halo), halo)
    prev_blk = p_ref[pl.ds(prev_start, halo), :].astype(F32)
    next_blk = p_ref[pl.ds(next_start, halo), :].astype(F32)
    prev_row = jnp.where(r0 > 0, prev_blk[halo - 1:halo, :], 0.0)
    next_row = jnp.where(r0 + rows < seq_len, next_blk[0:1, :], 0.0)
    row = lax.broadcasted_iota(jnp.int32, main.shape, 0)
    up = jnp.where(row == 0, prev_row, pltpu.roll(main, 1, 0))
    dn = jnp.where(row == rows - 1, next_row, pltpu.roll(main, rows - 1, 0))
    w = w_ref[...]
    return w[0:1] * up + w[1:2] * main + w[2:3] * dn + b_ref[...]


def _hyena_kernel(plan, x0_ref, x1_ref, v_ref, cw0_ref, cw1_ref, cw2_ref, cb0_ref, cb1_ref, cb2_ref,
                  skip_ref, ks_ref, w1_ref, w3_ref, w3t_ref, mi_ref, o_ref, u_ref, a_ref, b_ref):
    seq_len, n2, nb, h1, hp = plan.seq_len, plan.n2, plan.nb, plan.h1, plan.hp
    rows = CONV_ROWS
    nslab = u_ref.shape[0]

    def conv_u(c, carry):
        r0 = pl.multiple_of(c * rows, rows)
        x1c = _short_conv_rows(x1_ref, cw1_ref, cb1_ref, r0, rows, seq_len)
        vc = _short_conv_rows(v_ref, cw2_ref, cb2_ref, r0, rows, seq_len)
        _slab_store_block(u_ref, r0, rows, x1c * vc)
        return carry

    lax.fori_loop(0, seq_len // rows, conv_u, 0)

    _fft_stage1(plan, u_ref, a_ref, w1_ref)

    zpad = jnp.zeros(((hp - h1) * n2, LANES), F32)
    for s in range(nslab):
        b_ref[s, h1 * n2:hp * n2, :] = zpad
        b_ref[s, (hp + h1) * n2:2 * hp * n2, :] = zpad

    def per_k1(k1, carry):
        x = _fft_stage2(plan, a_ref, w3_ref, k1)
        k = ks_ref[k1].astype(F32)
        xr, xi, kr, ki = x[:n2], x[n2:], k[:n2], k[n2:]
        y = jnp.concatenate([xr * kr - xi * ki, xr * ki + xi * kr], axis=0).astype(BF16)
        bm = jnp.dot(w3t_ref[k1], y, preferred_element_type=F32)
        _slab_store_block(b_ref, pl.multiple_of(k1 * n2, n2), n2, bm[:n2])
        _slab_store_block(b_ref, pl.multiple_of((hp + k1) * n2, n2), n2, bm[n2:])
        return carry

    lax.fori_loop(0, h1, per_k1, 0)

    mi = mi_ref[...]
    skip = skip_ref[...]

    def last_stage(m, carry):
        r = _slab_load(b_ref, m, 2 * hp, n2).astype(BF16)
        y = jnp.dot(mi, r, preferred_element_type=F32)
        ucur = _slab_load(u_ref, m, nb, n2)
        _slab_store(u_ref, m, nb, n2, y + skip * ucur)
        return carry

    lax.fori_loop(0, n2, last_stage, 0, unroll=8)

    def gate_out(c, carry):
        r0 = pl.multiple_of(c * rows, rows)
        x0c = _short_conv_rows(x0_ref, cw0_ref, cb0_ref, r0, rows, seq_len)
        conv = jnp.concatenate([u_ref[s, pl.ds(r0, rows), :] for s in range(nslab)], axis=1)
        o_ref[pl.ds(r0, rows), :] = (x0c * conv).astype(o_ref.dtype)
        return carry

    lax.fori_loop(0, seq_len // rows, gate_out, 0)


def _hyena_operator(plan, proj3, conv_w, conv_b, skip, kspec, ct):
    bsz, seq_len, _ = proj3.shape
    c = skip.shape[-1]
    assert c % ct == 0 and ct % LANES == 0 and seq_len % CONV_ROWS == 0
    nct = c // ct
    nslab = ct // LANES
    n2, h1, pa, hp = plan.n2, plan.h1, plan.pa, plan.hp
    cb = conv_b.reshape(1, -1)
    part = lambda k: pl.BlockSpec((None, seq_len, ct), lambda j, b, k=k: (b, 0, k * nct + j))
    wpart = lambda k, r: pl.BlockSpec((r, ct), lambda j, b, k=k: (0, k * nct + j))
    return pl.pallas_call(
        functools.partial(_hyena_kernel, plan),
        grid=(nct, bsz),
        in_specs=[
            part(0), part(1), part(2),
            wpart(0, 3), wpart(1, 3), wpart(2, 3),
            wpart(0, 1), wpart(1, 1), wpart(2, 1),
            pl.BlockSpec((1, ct), lambda j, b: (0, j)),
            pl.BlockSpec((h1, 2 * n2, ct), lambda j, b: (0, 0, j), pipeline_mode=pl.Buffered(1)),
            _const_spec(plan.w1.shape),
            _const_spec(plan.w3.shape),
            _const_spec(plan.w3t.shape),
            _const_spec(plan.mi.shape),
        ],
        out_specs=pl.BlockSpec((None, seq_len, ct), lambda j, b: (b, 0, j)),
        out_shape=jax.ShapeDtypeStruct((bsz, seq_len, c), BF16),
        scratch_shapes=[
            pltpu.VMEM((nslab, seq_len, LANES), F32),
            pltpu.VMEM((nslab, n2 * pa, LANES), F32),
            pltpu.VMEM((nslab, 2 * hp * n2, LANES), F32),
        ],
        compiler_params=_cparams(2),
        name="hyena_operator",
    )(proj3, proj3, proj3, conv_w, conv_w, conv_w, cb, cb, cb, skip.reshape(1, -1), kspec,
      plan.w1, plan.w3, plan.w3t, plan.mi)


def _merge_kernel(us_ref, vs_ref, ga_ref, gb_ref, ya_ref, x_ref, sg_ref, sw_ref, sbt_ref,
                  pa_ref, pb_ref, wo_ref, o_ref, yb_ref):
    heads, chunk, _ = sw_ref.shape
    tm, width = vs_ref.shape
    hd = width // heads
    v = vs_ref[...].astype(F32)
    ms = jnp.mean(v * v, axis=-1, keepdims=True)
    vn_all = (v * lax.rsqrt(ms + EPS) * sg_ref[...]).astype(BF16)
    for r in range(tm // chunk):
        rs = slice(r * chunk, (r + 1) * chunk)
        for g in range(heads):
            cs = slice(g * hd, (g + 1) * hd)
            s = jnp.dot(sw_ref[g], vn_all[rs, cs], preferred_element_type=F32) + sbt_ref[:, g:g + 1]
            yb_ref[rs, cs] = (us_ref[rs, cs].astype(F32) * s).astype(BF16)
    pa = jnp.dot(ya_ref[...], pa_ref[...], preferred_element_type=F32)
    pb = jnp.dot(yb_ref[...], pb_ref[...], preferred_element_type=F32)
    merged = ga_ref[...].astype(F32) * pa + gb_ref[...].astype(F32) * pb
    o_ref[...] = x_ref[...] + jnp.dot(merged.astype(BF16), wo_ref[...], preferred_element_type=F32)


def _merge(proj, ya, x2d, sgu_g, sgu_w_bf16, sgu_b, pa_w, pb_w, wo_w, hy_cols, tm):
    t, d = x2d.shape
    width = sgu_g.shape[-1]
    heads, chunk, _ = sgu_w_bf16.shape
    assert t % tm == 0 and tm % chunk == 0 and hy_cols % width == 0 and ya.shape[1] == width
    base = hy_cols // width
    col = lambda k: pl.BlockSpec((tm, width), lambda i, k=k: (i, base + k))
    sbt = sgu_b.T
    return pl.pallas_call(
        _merge_kernel,
        grid=(t // tm,),
        in_specs=[
            col(0), col(1), col(2), col(3),
            pl.BlockSpec((tm, width), lambda i: (i, 0)),
            pl.BlockSpec((tm, d), lambda i: (i, 0)),
            _const_spec((1, width)),
            _const_spec(sgu_w_bf16.shape),
            _const_spec(sbt.shape),
            _const_spec(pa_w.shape),
            _const_spec(pb_w.shape),
            _const_spec(wo_w.shape),
        ],
        out_specs=pl.BlockSpec((tm, d), lambda i: (i, 0)),
        out_shape=jax.ShapeDtypeStruct((t, d), F32),
        scratch_shapes=[pltpu.VMEM((tm, width), BF16)],
        compiler_params=_cparams(1),
        name="sgu_merge",
    )(proj, proj, proj, proj, ya, x2d, sgu_g.reshape(1, -1), sgu_w_bf16, sbt, pa_w, pb_w, wo_w)


FFN_HALO = 16
FFN_CHUNK = 256


def _ffn_kernel(xm_ref, xp_ref, xn_ref, g_ref, wup_ref, cw_ref, cb_ref, wdn_ref, fg_ref, o_ref,
                hs_ref, acc_ref, *, tiles_per_seq, final_norm):
    tm, _ = xm_ref.shape
    hidden = wdn_ref.shape[0]
    halo = FFN_HALO
    i = pl.program_id(0)
    g = g_ref[...]

    def norm(xv):
        ms = jnp.mean(xv * xv, axis=-1, keepdims=True)
        return xv * lax.rsqrt(ms + EPS) * g

    at_start = (i % tiles_per_seq) == 0
    at_end = (i % tiles_per_seq) == tiles_per_seq - 1
    hs_ref[0:halo, :] = jnp.where(at_start, 0.0, norm(xp_ref[...])).astype(BF16)
    hs_ref[halo:halo + tm, :] = norm(xm_ref[...]).astype(BF16)
    hs_ref[halo + tm:, :] = jnp.where(at_end, 0.0, norm(xn_ref[...])).astype(BF16)

    acc_ref[...] = jnp.zeros_like(acc_ref)
    ext = tm + 2 * halo
    for k in range(hidden // FFN_CHUNK):
        cs = slice(k * FFN_CHUNK, (k + 1) * FFN_CHUNK)
        gs = slice(hidden + k * FFN_CHUNK, hidden + (k + 1) * FFN_CHUNK)
        a_ext = jnp.dot(hs_ref[...], wup_ref[:, cs], preferred_element_type=F32)
        gate = jnp.dot(hs_ref[halo:halo + tm, :], wup_ref[:, gs], preferred_element_type=F32)
        up = pltpu.roll(a_ext, 1, 0)[halo:halo + tm]
        dn = pltpu.roll(a_ext, ext - 1, 0)[halo:halo + tm]
        w = cw_ref[:, cs]
        c = w[0:1] * up + w[1:2] * a_ext[halo:halo + tm] + w[2:3] * dn + cb_ref[:, cs]
        act = (c / (1.0 + jnp.exp(-c)) * gate).astype(BF16)
        acc_ref[...] += jnp.dot(act, wdn_ref[cs, :], preferred_element_type=F32)
    y = xm_ref[...] + acc_ref[...]
    if final_norm:
        ms = jnp.mean(y * y, axis=-1, keepdims=True)
        y = y * lax.rsqrt(ms + EPS) * fg_ref[...]
    o_ref[...] = y


def _ffn(x2d, seq_len, g, wup_bf16, conv_w, conv_b, wdn_bf16, final_g, final_norm, tm):
    t, d = x2d.shape
    hidden = wdn_bf16.shape[0]
    halo = FFN_HALO
    assert t % tm == 0 and seq_len % tm == 0 and tm % halo == 0 and hidden % FFN_CHUNK == 0
    per = tm // halo
    last = t // halo - 1
    kern = functools.partial(_ffn_kernel, tiles_per_seq=seq_len // tm, final_norm=final_norm)
    return pl.pallas_call(
        kern,
        grid=(t // tm,),
        in_specs=[
            pl.BlockSpec((tm, d), lambda i: (i, 0)),
            pl.BlockSpec((halo, d), lambda i: (jnp.maximum(i * per - 1, 0), 0)),
            pl.BlockSpec((halo, d), lambda i: (jnp.minimum((i + 1) * per, last), 0)),
            _const_spec((1, d)),
            _const_spec(wup_bf16.shape),
            _const_spec(conv_w.shape),
            _const_spec((1, hidden)),
            _const_spec(wdn_bf16.shape),
            _const_spec((1, d)),
        ],
        out_specs=pl.BlockSpec((tm, d), lambda i: (i, 0)),
        out_shape=jax.ShapeDtypeStruct((t, d), F32),
        scratch_shapes=[pltpu.VMEM((tm + 2 * halo, d), BF16), pltpu.VMEM((tm, d), F32)],
        compiler_params=_cparams(1),
        name="ffn",
    )(x2d, x2d, x2d, g.reshape(1, -1), wup_bf16, conv_w, conv_b.reshape(1, -1), wdn_bf16,
      final_g.reshape(1, -1))


def kernel(x, norm1_g, w_in, hy_conv_w, hy_conv_b, filt_w1, filt_b1, filt_w2, filt_b2, filt_w3, filt_b3, filt_freq, filt_w4, hy_decay, hy_skip, sgu_norm_g, sgu_w, sgu_b, w_proj_hyena, w_proj_sgu, w_out, norm2_g, w_up, ffn_conv_w, ffn_conv_b, w_down, final_g):
    bsz, seq_len, d = x.shape
    depth = norm1_g.shape[0]
    hw = hy_skip.shape[-1]
    sw = sgu_norm_g.shape[-1]
    plan = _FftPlan(seq_len)
    x2d = x.reshape(bsz * seq_len, d)
    for i in range(depth):
        proj = _in_projection(x2d, norm1_g[i].reshape(1, -1), w_in[i].astype(BF16),
                              n_raw_cols=3 * hw, n_gelu_cols=2 * sw, tm=1024, tn=1024)
        filt = _implicit_filters(seq_len, filt_w1[i], filt_b1[i], filt_w2[i], filt_b2[i], filt_w3[i],
                                 filt_b3[i], filt_freq[i], filt_w4[i], hy_decay[i])
        kspec = _filter_spectrum(plan, filt, HYENA_CH)
        ya = _hyena_operator(plan, proj.reshape(bsz, seq_len, -1), hy_conv_w[i], hy_conv_b[i],
                             hy_skip[i], kspec, HYENA_CH)
        x2d = _merge(proj, ya.reshape(bsz * seq_len, hw), x2d, sgu_norm_g[i], sgu_w[i].astype(BF16),
                     sgu_b[i], w_proj_hyena[i].astype(BF16), w_proj_sgu[i].astype(BF16),
                     w_out[i].astype(BF16), hy_cols=3 * hw, tm=512)
        x2d = _ffn(x2d, seq_len, norm2_g[i], w_up[i].astype(BF16), ffn_conv_w[i], ffn_conv_b[i],
                   w_down[i].astype(BF16), final_g, final_norm=(i == depth - 1), tm=512)
    return x2d.reshape(bsz, seq_len, d)
```

```python
import functools
import math

import numpy as np
import jax
import jax.numpy as jnp
from jax import lax
from jax.experimental import pallas as pl
from jax.experimental.pallas import tpu as pltpu

EPS = 1e-6
F32 = jnp.float32
BF16 = jnp.bfloat16

LANES = 128
F32_SUBLANE_TILE = 8
BF16_SUBLANE_TILE = 16
VMEM_LIMIT_BYTES = 60 * 1024 * 1024

FFT_INNER = 128
CONV_ROWS = 256
HYENA_CH = 256
INPROJ_ROWS = 256


def _round_up(n, m):
    return -(-n // m) * m


def _cparams(n_axes):
    return pltpu.CompilerParams(
        dimension_semantics=("arbitrary",) * n_axes, vmem_limit_bytes=VMEM_LIMIT_BYTES)


def _const_spec(shape):
    nd = len(shape)
    return pl.BlockSpec(shape, lambda *_: (0,) * nd, pipeline_mode=pl.Buffered(1))


def _inproj_kernel(x_ref, g_ref, w_ref, o_ref, hn_ref, *, n_raw, n_gelu):
    j = pl.program_id(1)

    @pl.when(j == 0)
    def _():
        xv = x_ref[...]
        ms = jnp.mean(xv * xv, axis=-1, keepdims=True)
        hn_ref[...] = (xv * lax.rsqrt(ms + EPS) * g_ref[...]).astype(BF16)

    def project(epilogue):
        tm = o_ref.shape[0]
        for r in range(tm // INPROJ_ROWS):
            rs = slice(r * INPROJ_ROWS, (r + 1) * INPROJ_ROWS)
            acc = jnp.dot(hn_ref[rs, :], w_ref[...], preferred_element_type=F32)
            o_ref[rs, :] = epilogue(acc).astype(o_ref.dtype)

    @pl.when(j < n_raw)
    def _():
        project(lambda acc: acc)

    @pl.when(jnp.logical_and(j >= n_raw, j < n_raw + n_gelu))
    def _():
        project(lambda acc: 0.5 * acc * (1.0 + lax.erf(acc * math.sqrt(0.5))))

    @pl.when(j >= n_raw + n_gelu)
    def _():
        project(lambda acc: 1.0 / (1.0 + jnp.exp(-acc)))


def _in_projection(x2d, g, w_bf16, n_raw_cols, n_gelu_cols, tm, tn):
    t, d = x2d.shape
    n = w_bf16.shape[1]
    assert t % tm == 0 and n % tn == 0 and n_raw_cols % tn == 0 and n_gelu_cols % tn == 0
    kern = functools.partial(_inproj_kernel, n_raw=n_raw_cols // tn, n_gelu=n_gelu_cols // tn)
    return pl.pallas_call(
        kern,
        grid=(t // tm, n // tn),
        in_specs=[
            pl.BlockSpec((tm, d), lambda i, j: (i, 0)),
            pl.BlockSpec((1, d), lambda i, j: (0, 0)),
            pl.BlockSpec((d, tn), lambda i, j: (0, j)),
        ],
        out_specs=pl.BlockSpec((tm, tn), lambda i, j: (i, j)),
        out_shape=jax.ShapeDtypeStruct((t, n), BF16),
        scratch_shapes=[pltpu.VMEM((tm, d), BF16)],
        compiler_params=_cparams(2),
        name="in_projection",
    )(x2d, g, w_bf16)


def _filter_kernel(w1t_ref, w1c_ref, w1s_ref, b1_ref, w2_ref, b2_ref, w3_ref, b3_ref, fr_ref,
                   w4_ref, dec_ref, o_ref, *, seq_len, bands):
    tl = o_ref.shape[0]
    hi = lax.Precision.HIGHEST
    r0 = pl.program_id(0) * tl
    pos = (r0 + lax.broadcasted_iota(jnp.int32, (tl, 1), 0)).astype(F32)
    t = pos / float(seq_len - 1)
    band_step = (bands - 1 - 1e-4) / (bands - 1)
    band = 1e-4 + band_step * lax.broadcasted_iota(jnp.int32, (1, bands), 1).astype(F32)
    phase = (2.0 * math.pi / seq_len) * pos * band
    a = fr_ref[...]
    z1 = (t * w1t_ref[...]
          + jnp.dot(jnp.cos(phase), w1c_ref[...], preferred_element_type=F32, precision=hi)
          - jnp.dot(jnp.sin(phase), w1s_ref[...], preferred_element_type=F32, precision=hi))
    h = jnp.sin(a * (z1 + b1_ref[...]))
    h = jnp.sin(a * (jnp.dot(h, w2_ref[...], preferred_element_type=F32, precision=hi) + b2_ref[...]))
    h = jnp.sin(a * (jnp.dot(h, w3_ref[...], preferred_element_type=F32, precision=hi) + b3_ref[...]))
    f = jnp.dot(h, w4_ref[...], preferred_element_type=F32, precision=hi)
    o_ref[...] = f * jnp.exp(-t * jnp.abs(dec_ref[...]))


def _implicit_filters(seq_len, w1, b1, w2, b2, w3, b3, freq, w4, decay, tl=512):
    bands = (w1.shape[0] - 1) // 2
    hid = w1.shape[1]
    n_out = w4.shape[1]
    assert seq_len % tl == 0
    row = lambda v: v.reshape(1, -1)
    args = (w1[0:1], w1[1:1 + bands], w1[1 + bands:], row(b1), w2, row(b2), w3, row(b3), row(freq),
            w4, row(decay))
    full = lambda a: pl.BlockSpec(a.shape, lambda i: (0, 0))
    kern = functools.partial(_filter_kernel, seq_len=seq_len, bands=bands)
    del hid
    return pl.pallas_call(
        kern,
        grid=(seq_len // tl,),
        in_specs=[full(a) for a in args],
        out_specs=pl.BlockSpec((tl, n_out), lambda i: (i, 0)),
        out_shape=jax.ShapeDtypeStruct((seq_len, n_out), F32),
        compiler_params=_cparams(1),
        name="implicit_filter",
    )(*args)


class _FftPlan:
    def __init__(self, seq_len):
        n2 = FFT_INNER
        assert seq_len % n2 == 0
        self.seq_len = seq_len
        self.n = 2 * seq_len
        self.n2 = n2
        self.n1 = self.n // n2
        self.nb = seq_len // n2
        self.h1 = self.n1 // 2 + 1
        self.pa = _round_up(2 * self.h1, 8)
        self.hp = _round_up(self.h1, 8)
        self.pitch = n2 + 8
        assert (2 * self.hp) % BF16_SUBLANE_TILE == 0 and self.nb % BF16_SUBLANE_TILE == 0
        n1, nb, h1, n = self.n1, self.nb, self.h1, self.n
        k1 = np.arange(h1)[:, None]
        ph = 2.0 * np.pi * k1 * np.arange(nb)[None, :] / n1
        w1 = np.zeros((self.pa, nb))
        w1[:h1] = np.cos(ph)
        w1[h1:2 * h1] = -np.sin(ph)
        k2 = np.arange(n2)[:, None]
        m2 = np.arange(n2)[None, :]
        w3 = np.zeros((h1, 2 * n2, 2 * n2))
        for k in range(h1):
            th = 2.0 * np.pi * (((k + n1 * k2) * m2) % n) / n
            er, ei = np.cos(th), -np.sin(th)
            w3[k] = np.block([[er, -ei], [ei, er]])
        ck = np.full(h1, 2.0)
        ck[0] = 1.0
        ck[-1] = 1.0
        phi = 2.0 * np.pi * np.arange(nb)[:, None] * np.arange(h1)[None, :] / n1
        mi = np.zeros((nb, 2 * self.hp))
        mi[:, :h1] = ck * np.cos(phi) / n
        mi[:, self.hp:self.hp + h1] = -ck * np.sin(phi) / n
        self.w1 = jnp.asarray(w1, F32).astype(BF16)
        self.w3 = jnp.asarray(w3, F32).astype(BF16)
        self.w3t = jnp.asarray(np.transpose(w3, (0, 2, 1)), F32).astype(BF16)
        self.mi = jnp.asarray(mi, F32).astype(BF16)


def _slab_load(ref, start, size, stride):
    return jnp.concatenate(
        [ref[s, pl.ds(start, size, stride=stride), :] for s in range(ref.shape[0])], axis=1)


def _slab_store(ref, start, size, stride, val):
    for s in range(ref.shape[0]):
        ref[s, pl.ds(start, size, stride=stride), :] = val[:, s * LANES:(s + 1) * LANES]


def _slab_store_block(ref, start, size, val):
    for s in range(ref.shape[0]):
        ref[s, pl.ds(start, size), :] = val[:, s * LANES:(s + 1) * LANES]


def _fft_stage1(plan, u_ref, a_ref, w1_ref):
    w1 = w1_ref[...]

    def body(n2, carry):
        blk = _slab_load(u_ref, n2, plan.nb, plan.pitch).astype(BF16)
        a = jnp.dot(w1, blk, preferred_element_type=F32)
        _slab_store_block(a_ref, pl.multiple_of(n2 * plan.pa, 8), plan.pa, a)
        return carry

    lax.fori_loop(0, plan.n2, body, 0, unroll=16)


def _fft_stage2(plan, a_ref, w3_ref, k1):
    re = _slab_load(a_ref, k1, plan.n2, plan.pa)
    im = _slab_load(a_ref, plan.h1 + k1, plan.n2, plan.pa)
    rhs = jnp.concatenate([re, im], axis=0).astype(BF16)
    return jnp.dot(w3_ref[k1], rhs, preferred_element_type=F32)


def _spectrum_kernel(plan, ff_ref, fb_ref, w1_ref, w3_ref, o_ref, u_ref, af_ref, ab_ref):
    n2 = plan.n2
    nslab = u_ref.shape[0]
    for src, dst in ((ff_ref, af_ref), (fb_ref, ab_ref)):
        for blk in range(plan.nb):
            _slab_store_block(u_ref, blk * plan.pitch, n2, src[blk * n2:(blk + 1) * n2, :])
        _fft_stage1(plan, u_ref, dst, w1_ref)
    hb0 = fb_ref[0:1, :]

    def body(k1, carry):
        xf = _fft_stage2(plan, af_ref, w3_ref, k1)
        xb = _fft_stage2(plan, ab_ref, w3_ref, k1)
        o_ref[k1, 0:n2, :] = (xf[:n2] + xb[:n2] - hb0).astype(o_ref.dtype)
        o_ref[k1, n2:2 * n2, :] = (xf[n2:] - xb[n2:]).astype(o_ref.dtype)
        return carry

    lax.fori_loop(0, plan.h1, body, 0, unroll=3)


def _filter_spectrum(plan, filt, ct):
    seq_len, c2 = filt.shape
    c = c2 // 2
    assert c % ct == 0 and ct % LANES == 0
    nct = c // ct
    nslab = ct // LANES
    n2, h1, pa = plan.n2, plan.h1, plan.pa
    return pl.pallas_call(
        functools.partial(_spectrum_kernel, plan),
        grid=(nct,),
        in_specs=[
            pl.BlockSpec((seq_len, ct), lambda j: (0, j)),
            pl.BlockSpec((seq_len, ct), lambda j: (0, nct + j)),
            _const_spec(plan.w1.shape),
            _const_spec(plan.w3.shape),
        ],
        out_specs=pl.BlockSpec((h1, 2 * n2, ct), lambda j: (0, 0, j)),
        out_shape=jax.ShapeDtypeStruct((h1, 2 * n2, c), BF16),
        scratch_shapes=[
            pltpu.VMEM((nslab, plan.nb * plan.pitch, LANES), F32),
            pltpu.VMEM((nslab, n2 * pa, LANES), F32),
            pltpu.VMEM((nslab, n2 * pa, LANES), F32),
        ],
        compiler_params=_cparams(1),
        name="filter_spectrum",
    )(filt, filt, plan.w1, plan.w3)


def _short_conv_rows(p_ref, w_ref, b_ref, r0, rows, seq_len, stage_ref):
    halo = BF16_SUBLANE_TILE
    main = p_ref[pl.ds(r0, rows), :].astype(F32)
    prev_start = pl.multiple_of(jnp.maximum(r0 - halo, 0), halo)
    next_start = pl.multiple_of(jnp.minimum(r0 + rows, seq_len - halo), halo)
    prev_blk = p_ref[pl.ds(prev_start, halo), :].astype(F32)
    next_blk = p_ref[pl.ds(next_start, halo), :].astype(F32)
    pad = F32_SUBLANE_TILE
    prev_rows = jnp.where(r0 > 0, prev_blk[halo - pad:, :], 0.0)
    next_rows = jnp.where(r0 + rows < seq_len, next_blk[:pad, :], 0.0)
    _slab_store_block(stage_ref, 0, pad, prev_rows)
    _slab_store_block(stage_ref, pad, rows, main)
    _slab_store_block(stage_ref, pad + rows, pad, next_rows)
    nslab = stage_ref.shape[0]
    up = jnp.concatenate([stage_ref[s, pad - 1:pad - 1 + rows, :] for s in range(nslab)], axis=1)
    dn = jnp.concatenate([stage_ref[s, pad + 1:pad + 1 + rows, :] for s in range(nslab)], axis=1)
    w = w_ref[...]
    return w[0:1] * up + w[1:2] * main + w[2:3] * dn + b_ref[...]


def _hyena_kernel(plan, x0_ref, x1_ref, v_ref, cw0_ref, cw1_ref, cw2_ref, cb0_ref, cb1_ref, cb2_ref,
                  skip_ref, ks_ref, w1_ref, w3_ref, w3t_ref, mi_ref, o_ref, u_ref, a_ref, b_ref,
                  stage_ref):
    seq_len, n2, nb, h1, hp, pitch = plan.seq_len, plan.n2, plan.nb, plan.h1, plan.hp, plan.pitch
    rows = CONV_ROWS
    nslab = u_ref.shape[0]
    blocks = rows // n2

    def conv_u(c, carry):
        r0 = pl.multiple_of(c * rows, rows)
        x1c = _short_conv_rows(x1_ref, cw1_ref, cb1_ref, r0, rows, seq_len, stage_ref.at[0])
        vc = _short_conv_rows(v_ref, cw2_ref, cb2_ref, r0, rows, seq_len, stage_ref.at[1])
        u = x1c * vc
        for k in range(blocks):
            start = pl.multiple_of((c * blocks + k) * pitch, 8)
            _slab_store_block(u_ref, start, n2, u[k * n2:(k + 1) * n2])
        return carry

    lax.fori_loop(0, seq_len // rows, conv_u, 0)

    _fft_stage1(plan, u_ref, a_ref, w1_ref)

    zpad = jnp.zeros(((hp - h1) * pitch, LANES), F32)
    for s in range(nslab):
        b_ref[s, h1 * pitch:hp * pitch, :] = zpad
        b_ref[s, (hp + h1) * pitch:2 * hp * pitch, :] = zpad

    def per_k1(k1, carry):
        x = _fft_stage2(plan, a_ref, w3_ref, k1)
        k = ks_ref[k1].astype(F32)
        xr, xi, kr, ki = x[:n2], x[n2:], k[:n2], k[n2:]
        y = jnp.concatenate([xr * kr - xi * ki, xr * ki + xi * kr], axis=0).astype(BF16)
        bm = jnp.dot(w3t_ref[k1], y, preferred_element_type=F32)
        _slab_store_block(b_ref, pl.multiple_of(k1 * pitch, 8), n2, bm[:n2])
        _slab_store_block(b_ref, pl.multiple_of((hp + k1) * pitch, 8), n2, bm[n2:])
        return carry

    lax.fori_loop(0, h1, per_k1, 0, unroll=11)

    mi = mi_ref[...]
    skip = skip_ref[...]

    def last_stage(m, carry):
        r = _slab_load(b_ref, m, 2 * hp, pitch).astype(BF16)
        y = jnp.dot(mi, r, preferred_element_type=F32)
        ucur = _slab_load(u_ref, m, nb, pitch)
        _slab_store(u_ref, m, nb, pitch, y + skip * ucur)
        return carry

    lax.fori_loop(0, n2, last_stage, 0, unroll=16)

    def gate_out(c, carry):
        r0 = pl.multiple_of(c * rows, rows)
        x0c = _short_conv_rows(x0_ref, cw0_ref, cb0_ref, r0, rows, seq_len, stage_ref.at[0])
        conv = jnp.concatenate(
            [jnp.concatenate([u_ref[s, pl.ds(pl.multiple_of((c * blocks + k) * pitch, 8), n2), :]
                              for s in range(nslab)], axis=1) for k in range(blocks)], axis=0)
        o_ref[pl.ds(r0, rows), :] = (x0c * conv).astype(o_ref.dtype)
        return carry

    lax.fori_loop(0, seq_len // rows, gate_out, 0)


def _hyena_operator(plan, proj3, conv_w, conv_b, skip, kspec, ct):
    bsz, seq_len, _ = proj3.shape
    c = skip.shape[-1]
    assert c % ct == 0 and ct % LANES == 0 and seq_len % CONV_ROWS == 0
    nct = c // ct
    nslab = ct // LANES
    n2, h1, pa, hp = plan.n2, plan.h1, plan.pa, plan.hp
    cb = conv_b.reshape(1, -1)
    part = lambda k: pl.BlockSpec((None, seq_len, ct), lambda j, b, k=k: (b, 0, k * nct + j))
    wpart = lambda k, r: pl.BlockSpec((r, ct), lambda j, b, k=k: (0, k * nct + j))
    return pl.pallas_call(
        functools.partial(_hyena_kernel, plan),
        grid=(nct, bsz),
        in_specs=[
            part(0), part(1), part(2),
            wpart(0, 3), wpart(1, 3), wpart(2, 3),
            wpart(0, 1), wpart(1, 1), wpart(2, 1),
            pl.BlockSpec((1, ct), lambda j, b: (0, j)),
            pl.BlockSpec((h1, 2 * n2, ct), lambda j, b: (0, 0, j), pipeline_mode=pl.Buffered(1)),
            _const_spec(plan.w1.shape),
            _const_spec(plan.w3.shape),
            _const_spec(plan.w3t.shape),
            _const_spec(plan.mi.shape),
        ],
        out_specs=pl.BlockSpec((None, seq_len, ct), lambda j, b: (b, 0, j)),
        out_shape=jax.ShapeDtypeStruct((bsz, seq_len, c), BF16),
        scratch_shapes=[
            pltpu.VMEM((nslab, plan.nb * plan.pitch, LANES), F32),
            pltpu.VMEM((nslab, n2 * pa, LANES), F32),
            pltpu.VMEM((nslab, 2 * hp * plan.pitch, LANES), F32),
            pltpu.VMEM((2, nslab, CONV_ROWS + 2 * F32_SUBLANE_TILE, LANES), F32),
        ],
        compiler_params=_cparams(2),
        name="hyena_operator",
    )(proj3, proj3, proj3, conv_w, conv_w, conv_w, cb, cb, cb, skip.reshape(1, -1), kspec,
      plan.w1, plan.w3, plan.w3t, plan.mi)


def _merge_kernel(us_ref, vs_ref, ga_ref, gb_ref, ya_ref, x_ref, sg_ref, sw_ref, sbt_ref,
                  pa_ref, pb_ref, wo_ref, o_ref, yb_ref):
    heads, chunk, _ = sw_ref.shape
    tm, width = vs_ref.shape
    hd = width // heads
    v = vs_ref[...].astype(F32)
    ms = jnp.mean(v * v, axis=-1, keepdims=True)
    vn_all = (v * lax.rsqrt(ms + EPS) * sg_ref[...]).astype(BF16)
    for r in range(tm // chunk):
        rs = slice(r * chunk, (r + 1) * chunk)
        for g in range(heads):
            cs = slice(g * hd, (g + 1) * hd)
            s = jnp.dot(sw_ref[g], vn_all[rs, cs], preferred_element_type=F32) + sbt_ref[:, g:g + 1]
            yb_ref[rs, cs] = (us_ref[rs, cs].astype(F32) * s).astype(BF16)
    pa = jnp.dot(ya_ref[...], pa_ref[...], preferred_element_type=F32)
    pb = jnp.dot(yb_ref[...], pb_ref[...], preferred_element_type=F32)
    merged = ga_ref[...].astype(F32) * pa + gb_ref[...].astype(F32) * pb
    o_ref[...] = x_ref[...] + jnp.dot(merged.astype(BF16), wo_ref[...], preferred_element_type=F32)


def _merge(proj, ya, x2d, sgu_g, sgu_w_bf16, sgu_b, pa_w, pb_w, wo_w, hy_cols, tm):
    t, d = x2d.shape
    width = sgu_g.shape[-1]
    heads, chunk, _ = sgu_w_bf16.shape
    assert t % tm == 0 and tm % chunk == 0 and hy_cols % width == 0 and ya.shape[1] == width
    base = hy_cols // width
    col = lambda k: pl.BlockSpec((tm, width), lambda i, k=k: (i, base + k))
    sbt = sgu_b.T
    return pl.pallas_call(
        _merge_kernel,
        grid=(t // tm,),
        in_specs=[
            col(0), col(1), col(2), col(3),
            pl.BlockSpec((tm, width), lambda i: (i, 0)),
            pl.BlockSpec((tm, d), lambda i: (i, 0)),
            _const_spec((1, width)),
            _const_spec(sgu_w_bf16.shape),
            _const_spec(sbt.shape),
            _const_spec(pa_w.shape),
            _const_spec(pb_w.shape),
            _const_spec(wo_w.shape),
        ],
        out_specs=pl.BlockSpec((tm, d), lambda i: (i, 0)),
        out_shape=jax.ShapeDtypeStruct((t, d), F32),
        scratch_shapes=[pltpu.VMEM((tm, width), BF16)],
        compiler_params=_cparams(1),
        name="sgu_merge",
    )(proj, proj, proj, proj, ya, x2d, sgu_g.reshape(1, -1), sgu_w_bf16, sbt, pa_w, pb_w, wo_w)


FFN_HALO = 16
FFN_CHUNK = 256


def _ffn_kernel(xm_ref, xp_ref, xn_ref, g_ref, wup_ref, cw_ref, cb_ref, wdn_ref, fg_ref, o_ref,
                hs_ref, act_ref, *, tiles_per_seq, final_norm):
    tm, _ = xm_ref.shape
    hidden = wdn_ref.shape[0]
    halo = FFN_HALO
    i = pl.program_id(0)
    g = g_ref[...]

    def norm(xv):
        ms = jnp.mean(xv * xv, axis=-1, keepdims=True)
        return xv * lax.rsqrt(ms + EPS) * g

    at_start = (i % tiles_per_seq) == 0
    at_end = (i % tiles_per_seq) == tiles_per_seq - 1
    hs_ref[0:halo, :] = jnp.where(at_start, 0.0, norm(xp_ref[...])).astype(BF16)
    hs_ref[halo:halo + tm, :] = norm(xm_ref[...]).astype(BF16)
    hs_ref[halo + tm:, :] = jnp.where(at_end, 0.0, norm(xn_ref[...])).astype(BF16)

    ext = tm + 2 * halo
    for k in range(hidden // FFN_CHUNK):
        cs = slice(k * FFN_CHUNK, (k + 1) * FFN_CHUNK)
        gs = slice(hidden + k * FFN_CHUNK, hidden + (k + 1) * FFN_CHUNK)
        a_ext = jnp.dot(hs_ref[...], wup_ref[:, cs], preferred_element_type=F32)
        gate = jnp.dot(hs_ref[halo:halo + tm, :], wup_ref[:, gs], preferred_element_type=F32)
        up = pltpu.roll(a_ext, 1, 0)[halo:halo + tm]
        dn = pltpu.roll(a_ext, ext - 1, 0)[halo:halo + tm]
        w = cw_ref[:, cs]
        c = w[0:1] * up + w[1:2] * a_ext[halo:halo + tm] + w[2:3] * dn + cb_ref[:, cs]
        act_ref[:, cs] = (c / (1.0 + jnp.exp(-c)) * gate).astype(BF16)
    y = xm_ref[...] + jnp.dot(act_ref[...], wdn_ref[...], preferred_element_type=F32)
    if final_norm:
        ms = jnp.mean(y * y, axis=-1, keepdims=True)
        y = y * lax.rsqrt(ms + EPS) * fg_ref[...]
    o_ref[...] = y


def _ffn(x2d, seq_len, g, wup_bf16, conv_w, conv_b, wdn_bf16, final_g, final_norm, tm):
    t, d = x2d.shape
    hidden = wdn_bf16.shape[0]
    halo = FFN_HALO
    assert t % tm == 0 and seq_len % tm == 0 and tm % halo == 0 and hidden % FFN_CHUNK == 0
    per = tm // halo
    last = t // halo - 1
    kern = functools.partial(_ffn_kernel, tiles_per_seq=seq_len // tm, final_norm=final_norm)
    return pl.pallas_call(
        kern,
        grid=(t // tm,),
        in_specs=[
            pl.BlockSpec((tm, d), lambda i: (i, 0)),
            pl.BlockSpec((halo, d), lambda i: (jnp.maximum(i * per - 1, 0), 0)),
            pl.BlockSpec((halo, d), lambda i: (jnp.minimum((i + 1) * per, last), 0)),
            _const_spec((1, d)),
            _const_spec(wup_bf16.shape),
            _const_spec(conv_w.shape),
            _const_spec((1, hidden)),
            _const_spec(wdn_bf16.shape),
            _const_spec((1, d)),
        ],
        out_specs=pl.BlockSpec((tm, d), lambda i: (i, 0)),
        out_shape=jax.ShapeDtypeStruct((t, d), F32),
        scratch_shapes=[pltpu.VMEM((tm + 2 * halo, d), BF16), pltpu.VMEM((tm, hidden), BF16)],
        compiler_params=_cparams(1),
        name="ffn",
    )(x2d, x2d, x2d, g.reshape(1, -1), wup_bf16, conv_w, conv_b.reshape(1, -1), wdn_bf16,
      final_g.reshape(1, -1))


def kernel(x, norm1_g, w_in, hy_conv_w, hy_conv_b, filt_w1, filt_b1, filt_w2, filt_b2, filt_w3, filt_b3, filt_freq, filt_w4, hy_decay, hy_skip, sgu_norm_g, sgu_w, sgu_b, w_proj_hyena, w_proj_sgu, w_out, norm2_g, w_up, ffn_conv_w, ffn_conv_b, w_down, final_g):
    bsz, seq_len, d = x.shape
    depth = norm1_g.shape[0]
    hw = hy_skip.shape[-1]
    sw = sgu_norm_g.shape[-1]
    plan = _FftPlan(seq_len)
    x2d = x.reshape(bsz * seq_len, d)
    for i in range(depth):
        proj = _in_projection(x2d, norm1_g[i].reshape(1, -1), w_in[i].astype(BF16),
                              n_raw_cols=3 * hw, n_gelu_cols=2 * sw, tm=1024, tn=1024)
        filt = _implicit_filters(seq_len, filt_w1[i], filt_b1[i], filt_w2[i], filt_b2[i], filt_w3[i],
                                 filt_b3[i], filt_freq[i], filt_w4[i], hy_decay[i])
        kspec = _filter_spectrum(plan, filt, HYENA_CH)
        ya = _hyena_operator(plan, proj.reshape(bsz, seq_len, -1), hy_conv_w[i], hy_conv_b[i],
                             hy_skip[i], kspec, HYENA_CH)
        x2d = _merge(proj, ya.reshape(bsz * seq_len, hw), x2d, sgu_norm_g[i], sgu_w[i].astype(BF16),
                     sgu_b[i], w_proj_hyena[i].astype(BF16), w_proj_sgu[i].astype(BF16),
                     w_out[i].astype(BF16), hy_cols=3 * hw, tm=512)
        x2d = _ffn(x2d, seq_len, norm2_g[i], w_up[i].astype(BF16), ffn_conv_w[i], ffn_conv_b[i],
                   w_down[i].astype(BF16), final_g, final_norm=(i == depth - 1), tm=512)
    return x2d.reshape(bsz, seq_len, d)
```

```python
import functools
import math

import numpy as np
import jax
import jax.numpy as jnp
from jax import lax
from jax.experimental import pallas as pl
from jax.experimental.pallas import tpu as pltpu

EPS = 1e-6
F32 = jnp.float32
BF16 = jnp.bfloat16

LANES = 128
F32_SUBLANE_TILE = 8
BF16_SUBLANE_TILE = 16
VMEM_LIMIT_BYTES = 60 * 1024 * 1024

FFT_INNER = 128
CONV_ROWS = 256
HYENA_CH = 256


def _round_up(n, m):
    return -(-n // m) * m


def _cparams(n_axes):
    return pltpu.CompilerParams(
        dimension_semantics=("arbitrary",) * n_axes, vmem_limit_bytes=VMEM_LIMIT_BYTES)


def _const_spec(shape):
    nd = len(shape)
    return pl.BlockSpec(shape, lambda *_: (0,) * nd, pipeline_mode=pl.Buffered(1))


def _inproj_kernel(x_ref, g_ref, w_ref, o_ref, hn_ref, *, n_raw, n_gelu, tn):
    xv = x_ref[...]
    ms = jnp.mean(xv * xv, axis=-1, keepdims=True)
    hn_ref[...] = (xv * lax.rsqrt(ms + EPS) * g_ref[...]).astype(BF16)
    for j in range(o_ref.shape[1] // tn):
        cs = slice(j * tn, (j + 1) * tn)
        acc = jnp.dot(hn_ref[...], w_ref[:, cs], preferred_element_type=F32)
        if j < n_raw:
            out = acc
        elif j < n_raw + n_gelu:
            out = 0.5 * acc * (1.0 + lax.erf(acc * math.sqrt(0.5)))
        else:
            out = 1.0 / (1.0 + jnp.exp(-acc))
        o_ref[:, cs] = out.astype(o_ref.dtype)


def _in_projection(x2d, g, w_bf16, n_raw_cols, n_gelu_cols, tm, tn):
    t, d = x2d.shape
    n = w_bf16.shape[1]
    assert t % tm == 0 and n % tn == 0 and n_raw_cols % tn == 0 and n_gelu_cols % tn == 0
    kern = functools.partial(_inproj_kernel, n_raw=n_raw_cols // tn, n_gelu=n_gelu_cols // tn, tn=tn)
    return pl.pallas_call(
        kern,
        grid=(t // tm,),
        in_specs=[
            pl.BlockSpec((tm, d), lambda i: (i, 0)),
            _const_spec((1, d)),
            _const_spec(w_bf16.shape),
        ],
        out_specs=pl.BlockSpec((tm, n), lambda i: (i, 0)),
        out_shape=jax.ShapeDtypeStruct((t, n), BF16),
        scratch_shapes=[pltpu.VMEM((tm, d), BF16)],
        compiler_params=_cparams(1),
        name="in_projection",
    )(x2d, g, w_bf16)


def _filter_kernel(w1t_ref, w1c_ref, w1s_ref, b1_ref, w2_ref, b2_ref, w3_ref, b3_ref, fr_ref,
                   w4_ref, dec_ref, o_ref, *, seq_len, bands):
    tl = o_ref.shape[0]
    hi = lax.Precision.HIGHEST
    dot = functools.partial(jnp.dot, preferred_element_type=F32, precision=hi)
    r0 = pl.program_id(0) * tl
    pos = (r0 + lax.broadcasted_iota(jnp.int32, (1, tl), 1)).astype(F32)
    t = pos / float(seq_len - 1)
    band_step = (bands - 1 - 1e-4) / (bands - 1)
    band = 1e-4 + band_step * lax.broadcasted_iota(jnp.int32, (bands, 1), 0).astype(F32)
    phase = ((2.0 * math.pi / seq_len) * pos) * band
    a = fr_ref[...]
    z1 = w1t_ref[...] * t + dot(w1c_ref[...], jnp.cos(phase)) - dot(w1s_ref[...], jnp.sin(phase))
    h = jnp.sin(a * (z1 + b1_ref[...]))
    h = jnp.sin(a * (dot(w2_ref[...], h) + b2_ref[...]))
    h = jnp.sin(a * (dot(w3_ref[...], h) + b3_ref[...]))
    def split(v):
        v_hi = v.astype(BF16)
        return v_hi, (v - v_hi.astype(F32)).astype(BF16)

    h_hi, h_lo = split(h)
    w_hi, w_lo = split(w4_ref[...])
    lhs = jnp.concatenate([h_hi, h_lo, h_hi, h_lo], axis=0)
    rhs = jnp.concatenate([w_hi, w_hi, w_lo, w_lo], axis=0)
    f = lax.dot_general(lhs, rhs, (((0,), (0,)), ((), ())), preferred_element_type=F32)
    t_col = (r0 + lax.broadcasted_iota(jnp.int32, (tl, 1), 0)).astype(F32) / float(seq_len - 1)
    o_ref[...] = f * jnp.exp(-t_col * jnp.abs(dec_ref[...]))


def _implicit_filters(seq_len, w1, b1, w2, b2, w3, b3, freq, w4, decay, tl=512):
    bands = (w1.shape[0] - 1) // 2
    hid = w1.shape[1]
    n_out = w4.shape[1]
    assert seq_len % tl == 0
    col = lambda v: v.reshape(-1, 1)
    args = (w1[0:1].T, w1[1:1 + bands].T, w1[1 + bands:].T, col(b1), w2.T, col(b2), w3.T, col(b3),
            col(freq), w4, decay.reshape(1, -1))
    full = lambda a: pl.BlockSpec(a.shape, lambda i: (0, 0))
    kern = functools.partial(_filter_kernel, seq_len=seq_len, bands=bands)
    del hid
    return pl.pallas_call(
        kern,
        grid=(seq_len // tl,),
        in_specs=[full(a) for a in args],
        out_specs=pl.BlockSpec((tl, n_out), lambda i: (i, 0)),
        out_shape=jax.ShapeDtypeStruct((seq_len, n_out), F32),
        compiler_params=_cparams(1),
        name="implicit_filter",
    )(*args)


class _FftPlan:
    def __init__(self, seq_len):
        n2 = FFT_INNER
        assert seq_len % n2 == 0
        self.seq_len = seq_len
        self.n = 2 * seq_len
        self.n2 = n2
        self.n1 = self.n // n2
        self.nb = seq_len // n2
        self.h1 = self.n1 // 2 + 1
        self.pa = _round_up(2 * self.h1, 8)
        self.nq = self.n1
        self.dump_block = self.n1
        assert self.nb % BF16_SUBLANE_TILE == 0
        n1, nb, h1, n = self.n1, self.nb, self.h1, self.n
        k1 = np.arange(h1)[:, None]
        ph = 2.0 * np.pi * k1 * np.arange(nb)[None, :] / n1
        w1 = np.zeros((self.pa, nb))
        w1[:h1] = np.cos(ph)
        w1[h1:2 * h1] = -np.sin(ph)
        k2 = np.arange(n2)[:, None]
        m2 = np.arange(n2)[None, :]
        w3 = np.zeros((h1, 2 * n2, 2 * n2))
        for k in range(h1):
            th = 2.0 * np.pi * (((k + n1 * k2) * m2) % n) / n
            er, ei = np.cos(th), -np.sin(th)
            w3[k] = np.block([[er, -ei], [ei, er]])
        ck = np.full(h1, 2.0)
        ck[0] = 1.0
        ck[-1] = 1.0
        phi = 2.0 * np.pi * np.arange(nb)[:, None] * np.arange(h1)[None, :] / n1
        mi = np.zeros((nb, self.nq))
        mi[:, :h1] = ck * np.cos(phi) / n
        mi[:, h1:] = (-ck * np.sin(phi) / n)[:, 1:h1 - 1]
        g = F32_SUBLANE_TILE
        w1k = np.einsum("kn,ij->iknj", w1, np.eye(g)).reshape(g * self.pa, nb * g)
        self.w1 = jnp.asarray(w1k, F32).astype(BF16)
        self.w3 = jnp.asarray(w3, F32).astype(BF16)
        self.w3t = jnp.asarray(np.transpose(w3, (0, 2, 1)), F32).astype(BF16)
        mik = np.einsum("nq,ij->niqj", mi, np.eye(g)).reshape(nb * g, self.nq * g)
        self.mi = jnp.asarray(mik, F32).astype(BF16)


def _slab_load(ref, start, size, stride):
    return jnp.concatenate(
        [ref[s, pl.ds(start, size, stride=stride), :] for s in range(ref.shape[0])], axis=1)


def _slab_store_block(ref, start, size, val):
    for s in range(ref.shape[0]):
        ref[s, pl.ds(start, size), :] = val[:, s * LANES:(s + 1) * LANES]


def _fft_stage1(plan, u_ref, a_ref, w1_ref):
    w1 = w1_ref[...]
    g = F32_SUBLANE_TILE
    nslab = u_ref.shape[0]

    def body(c, carry):
        tiles = []
        for blk in range(plan.nb):
            start = pl.multiple_of(blk * plan.n2 + c * g, g)
            tiles.append(jnp.concatenate([u_ref[s, pl.ds(start, g), :] for s in range(nslab)], axis=1))
        rhs = jnp.concatenate(tiles, axis=0).astype(BF16)
        a = jnp.dot(w1, rhs, preferred_element_type=F32)
        _slab_store_block(a_ref, pl.multiple_of(c * g * plan.pa, g), g * plan.pa, a)
        return carry

    lax.fori_loop(0, plan.n2 // g, body, 0, unroll=8)


def _fft_stage2(plan, a_ref, w3_ref, k1):
    re = _slab_load(a_ref, k1, plan.n2, plan.pa)
    im = _slab_load(a_ref, plan.h1 + k1, plan.n2, plan.pa)
    rhs = jnp.concatenate([re, im], axis=0).astype(BF16)
    return jnp.dot(w3_ref[k1], rhs, preferred_element_type=F32)


def _spectrum_kernel(plan, ff_ref, fb_ref, w1_ref, w3_ref, o_ref, u_ref, af_ref, ab_ref):
    n2 = plan.n2
    nslab = u_ref.shape[0]
    for src, dst in ((ff_ref, af_ref), (fb_ref, ab_ref)):
        for s in range(nslab):
            u_ref[s] = src[:, s * LANES:(s + 1) * LANES]
        _fft_stage1(plan, u_ref, dst, w1_ref)
    hb0 = fb_ref[0:1, :]

    def body(k1, carry):
        xf = _fft_stage2(plan, af_ref, w3_ref, k1)
        xb = _fft_stage2(plan, ab_ref, w3_ref, k1)
        o_ref[k1, 0:n2, :] = (xf[:n2] + xb[:n2] - hb0).astype(o_ref.dtype)
        o_ref[k1, n2:2 * n2, :] = (xf[n2:] - xb[n2:]).astype(o_ref.dtype)
        return carry

    lax.fori_loop(0, plan.h1, body, 0, unroll=3)


def _filter_spectrum(plan, filt, ct):
    seq_len, c2 = filt.shape
    c = c2 // 2
    assert c % ct == 0 and ct % LANES == 0
    nct = c // ct
    nslab = ct // LANES
    n2, h1, pa = plan.n2, plan.h1, plan.pa
    return pl.pallas_call(
        functools.partial(_spectrum_kernel, plan),
        grid=(nct,),
        in_specs=[
            pl.BlockSpec((seq_len, ct), lambda j: (0, j)),
            pl.BlockSpec((seq_len, ct), lambda j: (0, nct + j)),
            _const_spec(plan.w1.shape),
            _const_spec(plan.w3.shape),
        ],
        out_specs=pl.BlockSpec((h1, 2 * n2, ct), lambda j: (0, 0, j)),
        out_shape=jax.ShapeDtypeStruct((h1, 2 * n2, c), BF16),
        scratch_shapes=[
            pltpu.VMEM((nslab, seq_len, LANES), F32),
            pltpu.VMEM((nslab, n2 * pa, LANES), F32),
            pltpu.VMEM((nslab, n2 * pa, LANES), F32),
        ],
        compiler_params=_cparams(1),
        name="filter_spectrum",
    )(filt, filt, plan.w1, plan.w3)


def _short_conv_rows(p_ref, w_ref, b_ref, r0, rows, seq_len, stage_ref):
    halo = BF16_SUBLANE_TILE
    main = p_ref[pl.ds(r0, rows), :].astype(F32)
    prev_start = pl.multiple_of(jnp.maximum(r0 - halo, 0), halo)
    next_start = pl.multiple_of(jnp.minimum(r0 + rows, seq_len - halo), halo)
    prev_blk = p_ref[pl.ds(prev_start, halo), :].astype(F32)
    next_blk = p_ref[pl.ds(next_start, halo), :].astype(F32)
    pad = F32_SUBLANE_TILE
    prev_rows = jnp.where(r0 > 0, prev_blk[halo - pad:, :], 0.0)
    next_rows = jnp.where(r0 + rows < seq_len, next_blk[:pad, :], 0.0)
    _slab_store_block(stage_ref, 0, pad, prev_rows)
    _slab_store_block(stage_ref, pad, rows, main)
    _slab_store_block(stage_ref, pad + rows, pad, next_rows)
    nslab = stage_ref.shape[0]
    up = jnp.concatenate([stage_ref[s, pad - 1:pad - 1 + rows, :] for s in range(nslab)], axis=1)
    dn = jnp.concatenate([stage_ref[s, pad + 1:pad + 1 + rows, :] for s in range(nslab)], axis=1)
    w = w_ref[...]
    return w[0:1] * up + w[1:2] * main + w[2:3] * dn + b_ref[...]


def _hyena_kernel(plan, x0_ref, x1_ref, v_ref, cw0_ref, cw1_ref, cw2_ref, cb0_ref, cb1_ref, cb2_ref,
                  skip_ref, ks_ref, w1_ref, w3_ref, w3t_ref, mi_ref, o_ref, u_ref, a_ref, b_ref,
                  stage_ref):
    seq_len, n2, nb, h1 = plan.seq_len, plan.n2, plan.nb, plan.h1
    rows = CONV_ROWS
    nslab = u_ref.shape[0]
    g = F32_SUBLANE_TILE

    def slab_rows(ref, start, size):
        return jnp.concatenate([ref[s, pl.ds(start, size), :] for s in range(nslab)], axis=1)

    def conv_u(c, carry):
        r0 = pl.multiple_of(c * rows, rows)
        x1c = _short_conv_rows(x1_ref, cw1_ref, cb1_ref, r0, rows, seq_len, stage_ref.at[0])
        vc = _short_conv_rows(v_ref, cw2_ref, cb2_ref, r0, rows, seq_len, stage_ref.at[1])
        _slab_store_block(u_ref, r0, rows, x1c * vc)
        return carry

    lax.fori_loop(0, seq_len // rows, conv_u, 0)

    _fft_stage1(plan, u_ref, a_ref, w1_ref)

    for k1 in range(h1):
        x = _fft_stage2(plan, a_ref, w3_ref, k1)
        k = ks_ref[k1].astype(F32)
        xr, xi, kr, ki = x[:n2], x[n2:], k[:n2], k[n2:]
        y = jnp.concatenate([xr * kr - xi * ki, xr * ki + xi * kr], axis=0).astype(BF16)
        bm = jnp.dot(w3t_ref[k1], y, preferred_element_type=F32)
        _slab_store_block(b_ref, k1 * n2, n2, bm[:n2])
        im_block = plan.dump_block if k1 in (0, h1 - 1) else h1 - 1 + k1
        _slab_store_block(b_ref, im_block * n2, n2, bm[n2:])

    mi = mi_ref[...]
    skip = skip_ref[...]

    def last_stage(c, carry):
        tiles = [slab_rows(b_ref, pl.multiple_of(q * n2 + c * g, g), g) for q in range(plan.nq)]
        rhs = jnp.concatenate(tiles, axis=0).astype(BF16)
        y = jnp.dot(mi, rhs, preferred_element_type=F32)
        for blk in range(nb):
            start = pl.multiple_of(blk * n2 + c * g, g)
            ucur = slab_rows(u_ref, start, g)
            _slab_store_block(u_ref, start, g, y[blk * g:(blk + 1) * g] + skip * ucur)
        return carry

    lax.fori_loop(0, n2 // g, last_stage, 0, unroll=4)

    def gate_out(c, carry):
        r0 = pl.multiple_of(c * rows, rows)
        x0c = _short_conv_rows(x0_ref, cw0_ref, cb0_ref, r0, rows, seq_len, stage_ref.at[0])
        o_ref[pl.ds(r0, rows), :] = (x0c * slab_rows(u_ref, r0, rows)).astype(o_ref.dtype)
        return carry

    lax.fori_loop(0, seq_len // rows, gate_out, 0)


def _hyena_operator(plan, proj3, conv_w, conv_b, skip, kspec, ct):
    bsz, seq_len, _ = proj3.shape
    c = skip.shape[-1]
    assert c % ct == 0 and ct % LANES == 0 and seq_len % CONV_ROWS == 0
    nct = c // ct
    nslab = ct // LANES
    n2, h1, pa = plan.n2, plan.h1, plan.pa
    cb = conv_b.reshape(1, -1)
    part = lambda k: pl.BlockSpec((None, seq_len, ct), lambda j, b, k=k: (b, 0, k * nct + j))
    wpart = lambda k, r: pl.BlockSpec((r, ct), lambda j, b, k=k: (0, k * nct + j))
    return pl.pallas_call(
        functools.partial(_hyena_kernel, plan),
        grid=(nct, bsz),
        in_specs=[
            part(0), part(1), part(2),
            wpart(0, 3), wpart(1, 3), wpart(2, 3),
            wpart(0, 1), wpart(1, 1), wpart(2, 1),
            pl.BlockSpec((1, ct), lambda j, b: (0, j)),
            pl.BlockSpec((h1, 2 * n2, ct), lambda j, b: (0, 0, j), pipeline_mode=pl.Buffered(1)),
            _const_spec(plan.w1.shape),
            _const_spec(plan.w3.shape),
            _const_spec(plan.w3t.shape),
            _const_spec(plan.mi.shape),
        ],
        out_specs=pl.BlockSpec((None, seq_len, ct), lambda j, b: (b, 0, j)),
        out_shape=jax.ShapeDtypeStruct((bsz, seq_len, c), BF16),
        scratch_shapes=[
            pltpu.VMEM((nslab, seq_len, LANES), F32),
            pltpu.VMEM((nslab, n2 * pa, LANES), F32),
            pltpu.VMEM((nslab, (plan.nq + 1) * n2, LANES), F32),
            pltpu.VMEM((2, nslab, CONV_ROWS + 2 * F32_SUBLANE_TILE, LANES), F32),
        ],
        compiler_params=_cparams(2),
        name="hyena_operator",
    )(proj3, proj3, proj3, conv_w, conv_w, conv_w, cb, cb, cb, skip.reshape(1, -1), kspec,
      plan.w1, plan.w3, plan.w3t, plan.mi)


def _merge_kernel(us_ref, vs_ref, ga_ref, gb_ref, ya_ref, x_ref, sg_ref, sw_ref, sbt_ref,
                  pa_ref, pb_ref, wo_ref, o_ref, yb_ref):
    heads, chunk, _ = sw_ref.shape
    tm, width = vs_ref.shape
    hd = width // heads
    v = vs_ref[...].astype(F32)
    ms = jnp.mean(v * v, axis=-1, keepdims=True)
    vn_all = (v * lax.rsqrt(ms + EPS) * sg_ref[...]).astype(BF16)
    for r in range(tm // chunk):
        rs = slice(r * chunk, (r + 1) * chunk)
        for g in range(heads):
            cs = slice(g * hd, (g + 1) * hd)
            s = jnp.dot(sw_ref[g], vn_all[rs, cs], preferred_element_type=F32) + sbt_ref[:, g:g + 1]
            yb_ref[rs, cs] = (us_ref[rs, cs].astype(F32) * s).astype(BF16)
    pa = jnp.dot(ya_ref[...], pa_ref[...], preferred_element_type=F32)
    pb = jnp.dot(yb_ref[...], pb_ref[...], preferred_element_type=F32)
    merged = ga_ref[...].astype(F32) * pa + gb_ref[...].astype(F32) * pb
    o_ref[...] = x_ref[...] + jnp.dot(merged.astype(BF16), wo_ref[...], preferred_element_type=F32)


def _merge(proj, ya, x2d, sgu_g, sgu_w_bf16, sgu_b, pa_w, pb_w, wo_w, hy_cols, tm):
    t, d = x2d.shape
    width = sgu_g.shape[-1]
    heads, chunk, _ = sgu_w_bf16.shape
    assert t % tm == 0 and tm % chunk == 0 and hy_cols % width == 0 and ya.shape[1] == width
    base = hy_cols // width
    col = lambda k: pl.BlockSpec((tm, width), lambda i, k=k: (i, base + k))
    sbt = sgu_b.T
    return pl.pallas_call(
        _merge_kernel,
        grid=(t // tm,),
        in_specs=[
            col(0), col(1), col(2), col(3),
            pl.BlockSpec((tm, width), lambda i: (i, 0)),
            pl.BlockSpec((tm, d), lambda i: (i, 0)),
            _const_spec((1, width)),
            _const_spec(sgu_w_bf16.shape),
            _const_spec(sbt.shape),
            _const_spec(pa_w.shape),
            _const_spec(pb_w.shape),
            _const_spec(wo_w.shape),
        ],
        out_specs=pl.BlockSpec((tm, d), lambda i: (i, 0)),
        out_shape=jax.ShapeDtypeStruct((t, d), F32),
        scratch_shapes=[pltpu.VMEM((tm, width), BF16)],
        compiler_params=_cparams(1),
        name="sgu_merge",
    )(proj, proj, proj, proj, ya, x2d, sgu_g.reshape(1, -1), sgu_w_bf16, sbt, pa_w, pb_w, wo_w)


FFN_HALO = 16
FFN_CHUNK = 256


def _ffn_kernel(xm_ref, xp_ref, xn_ref, g_ref, wup_ref, cw_ref, cb_ref, wdn_ref, fg_ref, o_ref,
                hs_ref, act_ref, *, tiles_per_seq, final_norm):
    tm, _ = xm_ref.shape
    hidden = wdn_ref.shape[0]
    halo = FFN_HALO
    i = pl.program_id(0)
    g = g_ref[...]

    def norm(xv):
        ms = jnp.mean(xv * xv, axis=-1, keepdims=True)
        return xv * lax.rsqrt(ms + EPS) * g

    at_start = (i % tiles_per_seq) == 0
    at_end = (i % tiles_per_seq) == tiles_per_seq - 1
    hs_ref[0:halo, :] = jnp.where(at_start, 0.0, norm(xp_ref[...])).astype(BF16)
    hs_ref[halo:halo + tm, :] = norm(xm_ref[...]).astype(BF16)
    hs_ref[halo + tm:, :] = jnp.where(at_end, 0.0, norm(xn_ref[...])).astype(BF16)

    ext = tm + 2 * halo
    for k in range(hidden // FFN_CHUNK):
        cs = slice(k * FFN_CHUNK, (k + 1) * FFN_CHUNK)
        gs = slice(hidden + k * FFN_CHUNK, hidden + (k + 1) * FFN_CHUNK)
        a_ext = jnp.dot(hs_ref[...], wup_ref[:, cs], preferred_element_type=F32)
        gate = jnp.dot(hs_ref[halo:halo + tm, :], wup_ref[:, gs], preferred_element_type=F32)
        up = pltpu.roll(a_ext, 1, 0)[halo:halo + tm]
        dn = pltpu.roll(a_ext, ext - 1, 0)[halo:halo + tm]
        w = cw_ref[:, cs]
        c = w[0:1] * up + w[1:2] * a_ext[halo:halo + tm] + w[2:3] * dn + cb_ref[:, cs]
        act_ref[:, cs] = (c / (1.0 + jnp.exp(-c)) * gate).astype(BF16)
    y = xm_ref[...] + jnp.dot(act_ref[...], wdn_ref[...], preferred_element_type=F32)
    if final_norm:
        ms = jnp.mean(y * y, axis=-1, keepdims=True)
        y = y * lax.rsqrt(ms + EPS) * fg_ref[...]
    o_ref[...] = y


def _ffn(x2d, seq_len, g, wup_bf16, conv_w, conv_b, wdn_bf16, final_g, final_norm, tm):
    t, d = x2d.shape
    hidden = wdn_bf16.shape[0]
    halo = FFN_HALO
    assert t % tm == 0 and seq_len % tm == 0 and tm % halo == 0 and hidden % FFN_CHUNK == 0
    per = tm // halo
    last = t // halo - 1
    kern = functools.partial(_ffn_kernel, tiles_per_seq=seq_len // tm, final_norm=final_norm)
    return pl.pallas_call(
        kern,
        grid=(t // tm,),
        in_specs=[
            pl.BlockSpec((tm, d), lambda i: (i, 0)),
            pl.BlockSpec((halo, d), lambda i: (jnp.maximum(i * per - 1, 0), 0)),
            pl.BlockSpec((halo, d), lambda i: (jnp.minimum((i + 1) * per, last), 0)),
            _const_spec((1, d)),
            _const_spec(wup_bf16.shape),
            _const_spec(conv_w.shape),
            _const_spec((1, hidden)),
            _const_spec(wdn_bf16.shape),
            _const_spec((1, d)),
        ],
        out_specs=pl.BlockSpec((tm, d), lambda i: (i, 0)),
        out_shape=jax.ShapeDtypeStruct((t, d), F32),
        scratch_shapes=[pltpu.VMEM((tm + 2 * halo, d), BF16), pltpu.VMEM((tm, hidden), BF16)],
        compiler_params=_cparams(1),
        name="ffn",
    )(x2d, x2d, x2d, g.reshape(1, -1), wup_bf16, conv_w, conv_b.reshape(1, -1), wdn_bf16,
      final_g.reshape(1, -1))


def kernel(x, norm1_g, w_in, hy_conv_w, hy_conv_b, filt_w1, filt_b1, filt_w2, filt_b2, filt_w3, filt_b3, filt_freq, filt_w4, hy_decay, hy_skip, sgu_norm_g, sgu_w, sgu_b, w_proj_hyena, w_proj_sgu, w_out, norm2_g, w_up, ffn_conv_w, ffn_conv_b, w_down, final_g):
    bsz, seq_len, d = x.shape
    depth = norm1_g.shape[0]
    hw = hy_skip.shape[-1]
    sw = sgu_norm_g.shape[-1]
    plan = _FftPlan(seq_len)
    x2d = x.reshape(bsz * seq_len, d)
    for i in range(depth):
        proj = _in_projection(x2d, norm1_g[i].reshape(1, -1), w_in[i].astype(BF16),
                              n_raw_cols=3 * hw, n_gelu_cols=2 * sw, tm=512, tn=512)
        filt = _implicit_filters(seq_len, filt_w1[i], filt_b1[i], filt_w2[i], filt_b2[i], filt_w3[i],
                                 filt_b3[i], filt_freq[i], filt_w4[i], hy_decay[i])
        kspec = _filter_spectrum(plan, filt, HYENA_CH)
        ya = _hyena_operator(plan, proj.reshape(bsz, seq_len, -1), hy_conv_w[i], hy_conv_b[i],
                             hy_skip[i], kspec, HYENA_CH)
        x2d = _merge(proj, ya.reshape(bsz * seq_len, hw), x2d, sgu_norm_g[i], sgu_w[i].astype(BF16),
                     sgu_b[i], w_proj_hyena[i].astype(BF16), w_proj_sgu[i].astype(BF16),
                     w_out[i].astype(BF16), hy_cols=3 * hw, tm=512)
        x2d = _ffn(x2d, seq_len, norm2_g[i], w_up[i].astype(BF16), ffn_conv_w[i], ffn_conv_b[i],
                   w_down[i].astype(BF16), final_g, final_norm=(i == depth - 1), tm=512)
    return x2d.reshape(bsz, seq_len, d)
```

```python
import functools
import math

import numpy as np
import jax
import jax.numpy as jnp
from jax import lax
from jax.experimental import pallas as pl
from jax.experimental.pallas import tpu as pltpu

EPS = 1e-6
F32 = jnp.float32
BF16 = jnp.bfloat16

LANES = 128
F32_SUBLANE_TILE = 8
BF16_SUBLANE_TILE = 16
VMEM_LIMIT_BYTES = 60 * 1024 * 1024

FFT_INNER = 128
CONV_ROWS = 256
HYENA_CH = 256


def _round_up(n, m):
    return -(-n // m) * m


def _cparams(n_axes):
    return pltpu.CompilerParams(
        dimension_semantics=("arbitrary",) * n_axes, vmem_limit_bytes=VMEM_LIMIT_BYTES)


def _const_spec(shape):
    nd = len(shape)
    return pl.BlockSpec(shape, lambda *_: (0,) * nd, pipeline_mode=pl.Buffered(1))


def _inproj_kernel(x_ref, g_ref, w_ref, o_ref, hn_ref, *, n_raw, n_gelu, tn):
    xv = x_ref[...]
    ms = jnp.mean(xv * xv, axis=-1, keepdims=True)
    hn_ref[...] = (xv * lax.rsqrt(ms + EPS) * g_ref[...]).astype(BF16)
    for j in range(o_ref.shape[1] // tn):
        cs = slice(j * tn, (j + 1) * tn)
        acc = jnp.dot(hn_ref[...], w_ref[:, cs].astype(BF16), preferred_element_type=F32)
        if j < n_raw:
            out = acc
        elif j < n_raw + n_gelu:
            out = 0.5 * acc * (1.0 + lax.erf(acc * math.sqrt(0.5)))
        else:
            out = 1.0 / (1.0 + jnp.exp(-acc))
        o_ref[:, cs] = out.astype(o_ref.dtype)


def _in_projection(x2d, g, w_in, n_raw_cols, n_gelu_cols, tm, tn):
    t, d = x2d.shape
    n = w_in.shape[1]
    assert t % tm == 0 and n % tn == 0 and n_raw_cols % tn == 0 and n_gelu_cols % tn == 0
    kern = functools.partial(_inproj_kernel, n_raw=n_raw_cols // tn, n_gelu=n_gelu_cols // tn, tn=tn)
    return pl.pallas_call(
        kern,
        grid=(t // tm,),
        in_specs=[
            pl.BlockSpec((tm, d), lambda i: (i, 0)),
            _const_spec((1, d)),
            _const_spec(w_in.shape),
        ],
        out_specs=pl.BlockSpec((tm, n), lambda i: (i, 0)),
        out_shape=jax.ShapeDtypeStruct((t, n), BF16),
        scratch_shapes=[pltpu.VMEM((tm, d), BF16)],
        compiler_params=_cparams(1),
        name="in_projection",
    )(x2d, g, w_in)


def _filter_kernel(w1t_ref, w1c_ref, w1s_ref, b1_ref, w2_ref, b2_ref, w3_ref, b3_ref, fr_ref,
                   w4_ref, dec_ref, o_ref, *, seq_len, bands):
    tl = o_ref.shape[0]
    hi = lax.Precision.HIGHEST
    dot = functools.partial(jnp.dot, preferred_element_type=F32, precision=hi)
    r0 = pl.program_id(0) * tl
    pos = (r0 + lax.broadcasted_iota(jnp.int32, (1, tl), 1)).astype(F32)
    t = pos / float(seq_len - 1)
    band_step = (bands - 1 - 1e-4) / (bands - 1)
    band = 1e-4 + band_step * lax.broadcasted_iota(jnp.int32, (bands, 1), 0).astype(F32)
    phase = ((2.0 * math.pi / seq_len) * pos) * band
    a = fr_ref[...]
    z1 = w1t_ref[...] * t + dot(w1c_ref[...], jnp.cos(phase)) - dot(w1s_ref[...], jnp.sin(phase))
    h = jnp.sin(a * (z1 + b1_ref[...]))
    h = jnp.sin(a * (dot(w2_ref[...], h) + b2_ref[...]))
    h = jnp.sin(a * (dot(w3_ref[...], h) + b3_ref[...]))
    def split(v):
        v_hi = v.astype(BF16)
        return v_hi, (v - v_hi.astype(F32)).astype(BF16)

    h_hi, h_lo = split(h)
    w_hi, w_lo = split(w4_ref[...])
    lhs = jnp.concatenate([h_hi, h_lo, h_hi, h_lo], axis=0)
    rhs = jnp.concatenate([w_hi, w_hi, w_lo, w_lo], axis=0)
    f = lax.dot_general(lhs, rhs, (((0,), (0,)), ((), ())), preferred_element_type=F32)
    t_col = (r0 + lax.broadcasted_iota(jnp.int32, (tl, 1), 0)).astype(F32) / float(seq_len - 1)
    o_ref[...] = f * jnp.exp(-t_col * jnp.abs(dec_ref[...]))


def _implicit_filters(seq_len, w1, b1, w2, b2, w3, b3, freq, w4, decay, tl=512):
    bands = (w1.shape[0] - 1) // 2
    hid = w1.shape[1]
    n_out = w4.shape[1]
    assert seq_len % tl == 0
    col = lambda v: v.reshape(-1, 1)
    args = (w1[0:1].T, w1[1:1 + bands].T, w1[1 + bands:].T, col(b1), w2.T, col(b2), w3.T, col(b3),
            col(freq), w4, decay.reshape(1, -1))
    full = lambda a: pl.BlockSpec(a.shape, lambda i: (0, 0))
    kern = functools.partial(_filter_kernel, seq_len=seq_len, bands=bands)
    del hid
    return pl.pallas_call(
        kern,
        grid=(seq_len // tl,),
        in_specs=[full(a) for a in args],
        out_specs=pl.BlockSpec((tl, n_out), lambda i: (i, 0)),
        out_shape=jax.ShapeDtypeStruct((seq_len, n_out), F32),
        compiler_params=_cparams(1),
        name="implicit_filter",
    )(*args)


class _FftPlan:
    def __init__(self, seq_len):
        n2 = FFT_INNER
        assert seq_len % n2 == 0
        self.seq_len = seq_len
        self.n = 2 * seq_len
        self.n2 = n2
        self.n1 = self.n // n2
        self.nb = seq_len // n2
        self.h1 = self.n1 // 2 + 1
        self.pa = _round_up(2 * self.h1, 8)
        self.nq = self.n1
        self.dump_block = self.n1
        assert self.nb % BF16_SUBLANE_TILE == 0
        n1, nb, h1, n = self.n1, self.nb, self.h1, self.n
        k1 = np.arange(h1)[:, None]
        ph = 2.0 * np.pi * k1 * np.arange(nb)[None, :] / n1
        w1 = np.zeros((self.pa, nb))
        w1[:h1] = np.cos(ph)
        w1[h1:2 * h1] = -np.sin(ph)
        k2 = np.arange(n2)[:, None]
        m2 = np.arange(n2)[None, :]
        w3 = np.zeros((h1, 2 * n2, 2 * n2))
        for k in range(h1):
            th = 2.0 * np.pi * (((k + n1 * k2) * m2) % n) / n
            er, ei = np.cos(th), -np.sin(th)
            w3[k] = np.block([[er, -ei], [ei, er]])
        ck = np.full(h1, 2.0)
        ck[0] = 1.0
        ck[-1] = 1.0
        phi = 2.0 * np.pi * np.arange(nb)[:, None] * np.arange(h1)[None, :] / n1
        mi = np.zeros((nb, self.nq))
        mi[:, :h1] = ck * np.cos(phi) / n
        mi[:, h1:] = (-ck * np.sin(phi) / n)[:, 1:h1 - 1]
        g = F32_SUBLANE_TILE
        w1k = np.einsum("kn,ij->iknj", w1, np.eye(g)).reshape(g * self.pa, nb * g)
        self.w1 = jnp.asarray(w1k, F32).astype(BF16)
        self.w3 = jnp.asarray(w3, F32).astype(BF16)
        self.w3t = jnp.asarray(np.transpose(w3, (0, 2, 1)), F32).astype(BF16)
        mik = np.einsum("nq,ij->niqj", mi, np.eye(g)).reshape(nb * g, self.nq * g)
        self.mi = jnp.asarray(mik, F32).astype(BF16)


def _slab_load(ref, start, size, stride):
    return jnp.concatenate(
        [ref[s, pl.ds(start, size, stride=stride), :] for s in range(ref.shape[0])], axis=1)


def _slab_store_block(ref, start, size, val):
    for s in range(ref.shape[0]):
        ref[s, pl.ds(start, size), :] = val[:, s * LANES:(s + 1) * LANES]


def _slab_rows(ref, start, size):
    return jnp.concatenate([ref[s, pl.ds(start, size), :] for s in range(ref.shape[0])], axis=1)


def _fft_stage1(plan, load_rows, a_ref, w1_ref):
    w1 = w1_ref[...]
    g = F32_SUBLANE_TILE

    def body(c, carry):
        tiles = [load_rows(pl.multiple_of(blk * plan.n2 + c * g, g), g) for blk in range(plan.nb)]
        rhs = jnp.concatenate(tiles, axis=0).astype(BF16)
        a = jnp.dot(w1, rhs, preferred_element_type=F32)
        _slab_store_block(a_ref, pl.multiple_of(c * g * plan.pa, g), g * plan.pa, a)
        return carry

    lax.fori_loop(0, plan.n2 // g, body, 0, unroll=8)


def _fft_stage2(plan, a_ref, w3_ref, k1):
    re = _slab_load(a_ref, k1, plan.n2, plan.pa)
    im = _slab_load(a_ref, plan.h1 + k1, plan.n2, plan.pa)
    rhs = jnp.concatenate([re, im], axis=0).astype(BF16)
    return jnp.dot(w3_ref[k1], rhs, preferred_element_type=F32)


def _spectrum_kernel(plan, ff_ref, fb_ref, w1_ref, w3_ref, o_ref, af_ref, ab_ref):
    n2 = plan.n2
    _fft_stage1(plan, lambda start, size: ff_ref[pl.ds(start, size), :], af_ref, w1_ref)
    _fft_stage1(plan, lambda start, size: fb_ref[pl.ds(start, size), :], ab_ref, w1_ref)
    hb0 = fb_ref[0:1, :]

    def body(k1, carry):
        xf = _fft_stage2(plan, af_ref, w3_ref, k1)
        xb = _fft_stage2(plan, ab_ref, w3_ref, k1)
        o_ref[k1, 0:n2, :] = (xf[:n2] + xb[:n2] - hb0).astype(o_ref.dtype)
        o_ref[k1, n2:2 * n2, :] = (xf[n2:] - xb[n2:]).astype(o_ref.dtype)
        return carry

    lax.fori_loop(0, plan.h1, body, 0, unroll=11)


def _filter_spectrum(plan, filt, ct):
    seq_len, c2 = filt.shape
    c = c2 // 2
    assert c % ct == 0 and ct % LANES == 0
    nct = c // ct
    nslab = ct // LANES
    n2, h1, pa = plan.n2, plan.h1, plan.pa
    return pl.pallas_call(
        functools.partial(_spectrum_kernel, plan),
        grid=(nct,),
        in_specs=[
            pl.BlockSpec((seq_len, ct), lambda j: (0, j)),
            pl.BlockSpec((seq_len, ct), lambda j: (0, nct + j)),
            _const_spec(plan.w1.shape),
            _const_spec(plan.w3.shape),
        ],
        out_specs=pl.BlockSpec((h1, 2 * n2, ct), lambda j: (0, 0, j)),
        out_shape=jax.ShapeDtypeStruct((h1, 2 * n2, c), BF16),
        scratch_shapes=[
            pltpu.VMEM((nslab, n2 * pa, LANES), F32),
            pltpu.VMEM((nslab, n2 * pa, LANES), F32),
        ],
        compiler_params=_cparams(1),
        name="filter_spectrum",
    )(filt, filt, plan.w1, plan.w3)


def _short_conv_rows(p_ref, w_ref, b_ref, r0, rows, seq_len, stage_ref):
    halo = BF16_SUBLANE_TILE
    main = p_ref[pl.ds(r0, rows), :].astype(F32)
    prev_start = pl.multiple_of(jnp.maximum(r0 - halo, 0), halo)
    next_start = pl.multiple_of(jnp.minimum(r0 + rows, seq_len - halo), halo)
    prev_blk = p_ref[pl.ds(prev_start, halo), :].astype(F32)
    next_blk = p_ref[pl.ds(next_start, halo), :].astype(F32)
    pad = F32_SUBLANE_TILE
    prev_rows = jnp.where(r0 > 0, prev_blk[halo - pad:, :], 0.0)
    next_rows = jnp.where(r0 + rows < seq_len, next_blk[:pad, :], 0.0)
    _slab_store_block(stage_ref, 0, pad, prev_rows)
    _slab_store_block(stage_ref, pad, rows, main)
    _slab_store_block(stage_ref, pad + rows, pad, next_rows)
    up = _slab_rows(stage_ref, pad - 1, rows)
    dn = _slab_rows(stage_ref, pad + 1, rows)
    w = w_ref[...]
    return w[0:1] * up + w[1:2] * main + w[2:3] * dn + b_ref[...]


def _hyena_kernel(plan, x0_ref, x1_ref, v_ref, cw0_ref, cw1_ref, cw2_ref, cb0_ref, cb1_ref, cb2_ref,
                  skip_ref, ks_ref, w1_ref, w3_ref, w3t_ref, mi_ref, o_ref, u_ref, a_ref, b_ref,
                  stage_ref):
    seq_len, n2, nb, h1 = plan.seq_len, plan.n2, plan.nb, plan.h1
    rows = CONV_ROWS
    g = F32_SUBLANE_TILE

    def conv_u(c, carry):
        r0 = pl.multiple_of(c * rows, rows)
        x1c = _short_conv_rows(x1_ref, cw1_ref, cb1_ref, r0, rows, seq_len, stage_ref.at[0])
        vc = _short_conv_rows(v_ref, cw2_ref, cb2_ref, r0, rows, seq_len, stage_ref.at[1])
        _slab_store_block(u_ref, r0, rows, x1c * vc)
        return carry

    lax.fori_loop(0, seq_len // rows, conv_u, 0)

    _fft_stage1(plan, functools.partial(_slab_rows, u_ref), a_ref, w1_ref)

    for k1 in range(h1):
        x = _fft_stage2(plan, a_ref, w3_ref, k1)
        k = ks_ref[k1].astype(F32)
        xr, xi, kr, ki = x[:n2], x[n2:], k[:n2], k[n2:]
        y = jnp.concatenate([xr * kr - xi * ki, xr * ki + xi * kr], axis=0).astype(BF16)
        bm = jnp.dot(w3t_ref[k1], y, preferred_element_type=F32)
        _slab_store_block(b_ref, k1 * n2, n2, bm[:n2])
        im_block = plan.dump_block if k1 in (0, h1 - 1) else h1 - 1 + k1
        _slab_store_block(b_ref, im_block * n2, n2, bm[n2:])

    mi = mi_ref[...]
    skip = skip_ref[...]

    def last_stage(c, carry):
        tiles = [_slab_rows(b_ref, pl.multiple_of(q * n2 + c * g, g), g) for q in range(plan.nq)]
        rhs = jnp.concatenate(tiles, axis=0).astype(BF16)
        y = jnp.dot(mi, rhs, preferred_element_type=F32)
        for blk in range(nb):
            start = pl.multiple_of(blk * n2 + c * g, g)
            ucur = _slab_rows(u_ref, start, g)
            _slab_store_block(u_ref, start, g, y[blk * g:(blk + 1) * g] + skip * ucur)
        return carry

    lax.fori_loop(0, n2 // g, last_stage, 0, unroll=4)

    def gate_out(c, carry):
        r0 = pl.multiple_of(c * rows, rows)
        x0c = _short_conv_rows(x0_ref, cw0_ref, cb0_ref, r0, rows, seq_len, stage_ref.at[0])
        o_ref[pl.ds(r0, rows), :] = (x0c * _slab_rows(u_ref, r0, rows)).astype(o_ref.dtype)
        return carry

    lax.fori_loop(0, seq_len // rows, gate_out, 0)


def _hyena_operator(plan, proj3, conv_w, conv_b, skip, kspec, ct):
    bsz, seq_len, _ = proj3.shape
    c = skip.shape[-1]
    assert c % ct == 0 and ct % LANES == 0 and seq_len % CONV_ROWS == 0
    nct = c // ct
    nslab = ct // LANES
    n2, h1, pa = plan.n2, plan.h1, plan.pa
    cb = conv_b.reshape(1, -1)
    part = lambda k: pl.BlockSpec((None, seq_len, ct), lambda j, b, k=k: (b, 0, k * nct + j))
    wpart = lambda k, r: pl.BlockSpec((r, ct), lambda j, b, k=k: (0, k * nct + j))
    return pl.pallas_call(
        functools.partial(_hyena_kernel, plan),
        grid=(nct, bsz),
        in_specs=[
            part(0), part(1), part(2),
            wpart(0, 3), wpart(1, 3), wpart(2, 3),
            wpart(0, 1), wpart(1, 1), wpart(2, 1),
            pl.BlockSpec((1, ct), lambda j, b: (0, j)),
            pl.BlockSpec((h1, 2 * n2, ct), lambda j, b: (0, 0, j), pipeline_mode=pl.Buffered(1)),
            _const_spec(plan.w1.shape),
            _const_spec(plan.w3.shape),
            _const_spec(plan.w3t.shape),
            _const_spec(plan.mi.shape),
        ],
        out_specs=pl.BlockSpec((None, seq_len, ct), lambda j, b: (b, 0, j)),
        out_shape=jax.ShapeDtypeStruct((bsz, seq_len, c), BF16),
        scratch_shapes=[
            pltpu.VMEM((nslab, seq_len, LANES), F32),
            pltpu.VMEM((nslab, n2 * pa, LANES), F32),
            pltpu.VMEM((nslab, (plan.nq + 1) * n2, LANES), F32),
            pltpu.VMEM((2, nslab, CONV_ROWS + 2 * F32_SUBLANE_TILE, LANES), F32),
        ],
        compiler_params=_cparams(2),
        name="hyena_operator",
    )(proj3, proj3, proj3, conv_w, conv_w, conv_w, cb, cb, cb, skip.reshape(1, -1), kspec,
      plan.w1, plan.w3, plan.w3t, plan.mi)


def _merge_kernel(us_ref, vs_ref, ga_ref, gb_ref, ya_ref, x_ref, sg_ref, sw_ref, sbt_ref,
                  pa_ref, pb_ref, wo_ref, o_ref, yb_ref):
    heads, chunk, _ = sw_ref.shape
    tm, width = vs_ref.shape
    hd = width // heads
    v = vs_ref[...].astype(F32)
    ms = jnp.mean(v * v, axis=-1, keepdims=True)
    vn_all = (v * lax.rsqrt(ms + EPS) * sg_ref[...]).astype(BF16)
    for r in range(tm // chunk):
        rs = slice(r * chunk, (r + 1) * chunk)
        for g in range(heads):
            cs = slice(g * hd, (g + 1) * hd)
            s = (jnp.dot(sw_ref[g].astype(BF16), vn_all[rs, cs], preferred_element_type=F32)
                 + sbt_ref[:, g:g + 1])
            yb_ref[rs, cs] = (us_ref[rs, cs].astype(F32) * s).astype(BF16)
    pa = jnp.dot(ya_ref[...], pa_ref[...].astype(BF16), preferred_element_type=F32)
    pb = jnp.dot(yb_ref[...], pb_ref[...].astype(BF16), preferred_element_type=F32)
    merged = ga_ref[...].astype(F32) * pa + gb_ref[...].astype(F32) * pb
    o_ref[...] = x_ref[...] + jnp.dot(merged.astype(BF16), wo_ref[...].astype(BF16),
                                      preferred_element_type=F32)


def _merge(proj, ya, x2d, sgu_g, sgu_w, sgu_b, pa_w, pb_w, wo_w, hy_cols, tm):
    t, d = x2d.shape
    width = sgu_g.shape[-1]
    heads, chunk, _ = sgu_w.shape
    assert t % tm == 0 and tm % chunk == 0 and hy_cols % width == 0 and ya.shape[1] == width
    base = hy_cols // width
    col = lambda k: pl.BlockSpec((tm, width), lambda i, k=k: (i, base + k))
    sbt = sgu_b.T
    return pl.pallas_call(
        _merge_kernel,
        grid=(t // tm,),
        in_specs=[
            col(0), col(1), col(2), col(3),
            pl.BlockSpec((tm, width), lambda i: (i, 0)),
            pl.BlockSpec((tm, d), lambda i: (i, 0)),
            _const_spec((1, width)),
            _const_spec(sgu_w.shape),
            _const_spec(sbt.shape),
            _const_spec(pa_w.shape),
            _const_spec(pb_w.shape),
            _const_spec(wo_w.shape),
        ],
        out_specs=pl.BlockSpec((tm, d), lambda i: (i, 0)),
        out_shape=jax.ShapeDtypeStruct((t, d), F32),
        scratch_shapes=[pltpu.VMEM((tm, width), BF16)],
        compiler_params=_cparams(1),
        name="sgu_merge",
    )(proj, proj, proj, proj, ya, x2d, sgu_g.reshape(1, -1), sgu_w, sbt, pa_w, pb_w, wo_w)


FFN_HALO = 16
FFN_CHUNK = 256
FFN_SUB_ROWS = 512


def _ffn_kernel(xm_ref, xp_ref, xn_ref, g_ref, wup_ref, cw_ref, cb_ref, wdn_ref, fg_ref, o_ref,
                hs_ref, act_ref, *, tiles_per_seq, final_norm):
    tm, _ = xm_ref.shape
    hidden = wdn_ref.shape[0]
    halo = FFN_HALO
    i = pl.program_id(0)
    g = g_ref[...]

    def norm(xv):
        ms = jnp.mean(xv * xv, axis=-1, keepdims=True)
        return xv * lax.rsqrt(ms + EPS) * g

    at_start = (i % tiles_per_seq) == 0
    at_end = (i % tiles_per_seq) == tiles_per_seq - 1
    hs_ref[0:halo, :] = jnp.where(at_start, 0.0, norm(xp_ref[...])).astype(BF16)
    hs_ref[halo + tm:, :] = jnp.where(at_end, 0.0, norm(xn_ref[...])).astype(BF16)

    ts = FFN_SUB_ROWS
    ext = ts + 2 * halo
    for s in range(tm // ts):
        rs = slice(s * ts, (s + 1) * ts)
        hs_ref[halo + s * ts:halo + (s + 1) * ts, :] = norm(xm_ref[rs, :]).astype(BF16)
    for s in range(tm // ts):
        rs = slice(s * ts, (s + 1) * ts)
        h_ext = hs_ref[s * ts:s * ts + ext, :]
        h_main = hs_ref[halo + s * ts:halo + (s + 1) * ts, :]
        for k in range(hidden // FFN_CHUNK):
            cs = slice(k * FFN_CHUNK, (k + 1) * FFN_CHUNK)
            gs = slice(hidden + k * FFN_CHUNK, hidden + (k + 1) * FFN_CHUNK)
            a_ext = jnp.dot(h_ext, wup_ref[:, cs].astype(BF16), preferred_element_type=F32)
            gate = jnp.dot(h_main, wup_ref[:, gs].astype(BF16), preferred_element_type=F32)
            up = pltpu.roll(a_ext, 1, 0)[halo:halo + ts]
            dn = pltpu.roll(a_ext, ext - 1, 0)[halo:halo + ts]
            w = cw_ref[:, cs]
            c = w[0:1] * up + w[1:2] * a_ext[halo:halo + ts] + w[2:3] * dn + cb_ref[:, cs]
            act_ref[rs, cs] = (c / (1.0 + jnp.exp(-c)) * gate).astype(BF16)
        y = xm_ref[rs, :] + jnp.dot(act_ref[rs, :], wdn_ref[...].astype(BF16),
                                    preferred_element_type=F32)
        if final_norm:
            ms = jnp.mean(y * y, axis=-1, keepdims=True)
            y = y * lax.rsqrt(ms + EPS) * fg_ref[...]
        o_ref[rs, :] = y


def _ffn(x2d, seq_len, g, wup, conv_w, conv_b, wdn, final_g, final_norm, tm):
    t, d = x2d.shape
    hidden = wdn.shape[0]
    halo = FFN_HALO
    assert t % tm == 0 and seq_len % tm == 0 and tm % FFN_SUB_ROWS == 0 and FFN_SUB_ROWS % halo == 0
    assert hidden % FFN_CHUNK == 0
    per = tm // halo
    last = t // halo - 1
    kern = functools.partial(_ffn_kernel, tiles_per_seq=seq_len // tm, final_norm=final_norm)
    return pl.pallas_call(
        kern,
        grid=(t // tm,),
        in_specs=[
            pl.BlockSpec((tm, d), lambda i: (i, 0)),
            pl.BlockSpec((halo, d), lambda i: (jnp.maximum(i * per - 1, 0), 0)),
            pl.BlockSpec((halo, d), lambda i: (jnp.minimum((i + 1) * per, last), 0)),
            _const_spec((1, d)),
            _const_spec(wup.shape),
            _const_spec(conv_w.shape),
            _const_spec((1, hidden)),
            _const_spec(wdn.shape),
            _const_spec((1, d)),
        ],
        out_specs=pl.BlockSpec((tm, d), lambda i: (i, 0)),
        out_shape=jax.ShapeDtypeStruct((t, d), F32),
        scratch_shapes=[pltpu.VMEM((tm + 2 * halo, d), BF16), pltpu.VMEM((tm, hidden), BF16)],
        compiler_params=_cparams(1),
        name="ffn",
    )(x2d, x2d, x2d, g.reshape(1, -1), wup, conv_w, conv_b.reshape(1, -1), wdn,
      final_g.reshape(1, -1))


def kernel(x, norm1_g, w_in, hy_conv_w, hy_conv_b, filt_w1, filt_b1, filt_w2, filt_b2, filt_w3, filt_b3, filt_freq, filt_w4, hy_decay, hy_skip, sgu_norm_g, sgu_w, sgu_b, w_proj_hyena, w_proj_sgu, w_out, norm2_g, w_up, ffn_conv_w, ffn_conv_b, w_down, final_g):
    bsz, seq_len, d = x.shape
    depth = norm1_g.shape[0]
    hw = hy_skip.shape[-1]
    sw = sgu_norm_g.shape[-1]
    plan = _FftPlan(seq_len)
    x2d = x.reshape(bsz * seq_len, d)
    for i in range(depth):
        proj = _in_projection(x2d, norm1_g[i].reshape(1, -1), w_in[i],
                              n_raw_cols=3 * hw, n_gelu_cols=2 * sw, tm=512, tn=512)
        filt = _implicit_filters(seq_len, filt_w1[i], filt_b1[i], filt_w2[i], filt_b2[i], filt_w3[i],
                                 filt_b3[i], filt_freq[i], filt_w4[i], hy_decay[i])
        kspec = _filter_spectrum(plan, filt, HYENA_CH)
        ya = _hyena_operator(plan, proj.reshape(bsz, seq_len, -1), hy_conv_w[i], hy_conv_b[i],
                             hy_skip[i], kspec, HYENA_CH)
        x2d = _merge(proj, ya.reshape(bsz * seq_len, hw), x2d, sgu_norm_g[i], sgu_w[i],
                     sgu_b[i], w_proj_hyena[i], w_proj_sgu[i], w_out[i], hy_cols=3 * hw, tm=512)
        x2d = _ffn(x2d, seq_len, norm2_g[i], w_up[i], ffn_conv_w[i], ffn_conv_b[i],
                   w_down[i], final_g, final_norm=(i == depth - 1), tm=1024)
    return x2d.reshape(bsz, seq_len, d)
```

```python
import functools
import math

import numpy as np
import jax
import jax.numpy as jnp
from jax import lax
from jax.experimental import pallas as pl
from jax.experimental.pallas import tpu as pltpu

EPS = 1e-6
F32 = jnp.float32
BF16 = jnp.bfloat16

LANES = 128
F32_SUBLANE_TILE = 8
BF16_SUBLANE_TILE = 16
VMEM_LIMIT_BYTES = 60 * 1024 * 1024

FFT_INNER = 128
CONV_ROWS = 256
HYENA_CH = 256


def _round_up(n, m):
    return -(-n // m) * m


def _cparams(n_axes):
    return pltpu.CompilerParams(
        dimension_semantics=("arbitrary",) * n_axes, vmem_limit_bytes=VMEM_LIMIT_BYTES)


def _const_spec(shape):
    nd = len(shape)
    return pl.BlockSpec(shape, lambda *_: (0,) * nd, pipeline_mode=pl.Buffered(1))


def _inproj_kernel(x_ref, g_ref, w_ref, o_ref, hn_ref, *, n_raw, n_gelu, tn):
    xv = x_ref[...]
    ms = jnp.mean(xv * xv, axis=-1, keepdims=True)
    hn_ref[...] = (xv * lax.rsqrt(ms + EPS) * g_ref[...]).astype(BF16)
    for j in range(o_ref.shape[1] // tn):
        cs = slice(j * tn, (j + 1) * tn)
        acc = jnp.dot(hn_ref[...], w_ref[:, cs].astype(BF16), preferred_element_type=F32)
        if j < n_raw:
            out = acc
        elif j < n_raw + n_gelu:
            out = 0.5 * acc * (1.0 + lax.erf(acc * math.sqrt(0.5)))
        else:
            out = 1.0 / (1.0 + jnp.exp(-acc))
        o_ref[:, cs] = out.astype(o_ref.dtype)


def _in_projection(x2d, g, w_in, n_raw_cols, n_gelu_cols, tm, tn):
    t, d = x2d.shape
    n = w_in.shape[1]
    assert t % tm == 0 and n % tn == 0 and n_raw_cols % tn == 0 and n_gelu_cols % tn == 0
    kern = functools.partial(_inproj_kernel, n_raw=n_raw_cols // tn, n_gelu=n_gelu_cols // tn, tn=tn)
    return pl.pallas_call(
        kern,
        grid=(t // tm,),
        in_specs=[
            pl.BlockSpec((tm, d), lambda i: (i, 0)),
            _const_spec((1, d)),
            _const_spec(w_in.shape),
        ],
        out_specs=pl.BlockSpec((tm, n), lambda i: (i, 0)),
        out_shape=jax.ShapeDtypeStruct((t, n), BF16),
        scratch_shapes=[pltpu.VMEM((tm, d), BF16)],
        compiler_params=_cparams(1),
        name="in_projection",
    )(x2d, g, w_in)


def _filter_kernel(w1t_ref, w1c_ref, w1s_ref, b1_ref, w2_ref, b2_ref, w3_ref, b3_ref, fr_ref,
                   w4_ref, dec_ref, o_ref, *, seq_len, bands):
    tl = o_ref.shape[0]
    hi = lax.Precision.HIGHEST
    dot = functools.partial(jnp.dot, preferred_element_type=F32, precision=hi)
    r0 = pl.program_id(0) * tl
    pos = (r0 + lax.broadcasted_iota(jnp.int32, (1, tl), 1)).astype(F32)
    t = pos / float(seq_len - 1)
    band_step = (bands - 1 - 1e-4) / (bands - 1)
    band = 1e-4 + band_step * lax.broadcasted_iota(jnp.int32, (bands, 1), 0).astype(F32)
    phase = ((2.0 * math.pi / seq_len) * pos) * band
    a = fr_ref[...]
    z1 = w1t_ref[...] * t + dot(w1c_ref[...], jnp.cos(phase)) - dot(w1s_ref[...], jnp.sin(phase))
    h = jnp.sin(a * (z1 + b1_ref[...]))
    h = jnp.sin(a * (dot(w2_ref[...], h) + b2_ref[...]))
    h = jnp.sin(a * (dot(w3_ref[...], h) + b3_ref[...]))
    def split(v):
        v_hi = v.astype(BF16)
        return v_hi, (v - v_hi.astype(F32)).astype(BF16)

    h_hi, h_lo = split(h)
    w_hi, w_lo = split(w4_ref[...])
    lhs = jnp.concatenate([h_hi, h_lo, h_hi, h_lo], axis=0)
    rhs = jnp.concatenate([w_hi, w_hi, w_lo, w_lo], axis=0)
    f = lax.dot_general(lhs, rhs, (((0,), (0,)), ((), ())), preferred_element_type=F32)
    t_col = (r0 + lax.broadcasted_iota(jnp.int32, (tl, 1), 0)).astype(F32) / float(seq_len - 1)
    o_ref[...] = f * jnp.exp(-t_col * jnp.abs(dec_ref[...]))


def _implicit_filters(seq_len, w1, b1, w2, b2, w3, b3, freq, w4, decay, tl=512):
    bands = (w1.shape[0] - 1) // 2
    hid = w1.shape[1]
    n_out = w4.shape[1]
    assert seq_len % tl == 0
    col = lambda v: v.reshape(-1, 1)
    args = (w1[0:1].T, w1[1:1 + bands].T, w1[1 + bands:].T, col(b1), w2.T, col(b2), w3.T, col(b3),
            col(freq), w4, decay.reshape(1, -1))
    full = lambda a: pl.BlockSpec(a.shape, lambda i: (0, 0))
    kern = functools.partial(_filter_kernel, seq_len=seq_len, bands=bands)
    del hid
    return pl.pallas_call(
        kern,
        grid=(seq_len // tl,),
        in_specs=[full(a) for a in args],
        out_specs=pl.BlockSpec((tl, n_out), lambda i: (i, 0)),
        out_shape=jax.ShapeDtypeStruct((seq_len, n_out), F32),
        compiler_params=_cparams(1),
        name="implicit_filter",
    )(*args)


class _FftPlan:
    def __init__(self, seq_len):
        n2 = FFT_INNER
        assert seq_len % n2 == 0
        self.seq_len = seq_len
        self.n = 2 * seq_len
        self.n2 = n2
        self.n1 = self.n // n2
        self.nb = seq_len // n2
        self.h1 = self.n1 // 2 + 1
        self.pa = 2 * self.h1
        self.nq = self.n1
        self.dump_block = self.n1
        assert self.nb % BF16_SUBLANE_TILE == 0
        n1, nb, h1, n = self.n1, self.nb, self.h1, self.n
        k1 = np.arange(h1)[:, None]
        ph = 2.0 * np.pi * k1 * np.arange(nb)[None, :] / n1
        w1 = np.zeros((self.pa, nb))
        w1[:h1] = np.cos(ph)
        w1[h1:2 * h1] = -np.sin(ph)
        k2 = np.arange(n2)[:, None]
        m2 = np.arange(n2)[None, :]
        w3 = np.zeros((h1, 2 * n2, 2 * n2))
        for k in range(h1):
            th = 2.0 * np.pi * (((k + n1 * k2) * m2) % n) / n
            er, ei = np.cos(th), -np.sin(th)
            w3[k] = np.block([[er, -ei], [ei, er]])
        ck = np.full(h1, 2.0)
        ck[0] = 1.0
        ck[-1] = 1.0
        phi = 2.0 * np.pi * np.arange(nb)[:, None] * np.arange(h1)[None, :] / n1
        mi = np.zeros((nb, self.nq))
        mi[:, :h1] = ck * np.cos(phi) / n
        mi[:, h1:] = (-ck * np.sin(phi) / n)[:, 1:h1 - 1]
        g = F32_SUBLANE_TILE
        w1k = np.einsum("kn,ij->kinj", w1, np.eye(g)).reshape(self.pa * g, nb * g)
        self.w1 = jnp.asarray(w1k, F32).astype(BF16)
        self.w3 = jnp.asarray(w3, F32).astype(BF16)
        self.w3t = jnp.asarray(np.transpose(w3, (0, 2, 1)), F32).astype(BF16)
        mik = np.einsum("nq,ij->niqj", mi, np.eye(g)).reshape(nb * g, self.nq * g)
        self.mi = jnp.asarray(mik, F32).astype(BF16)


def _slab_store_block(ref, start, size, val):
    for s in range(ref.shape[0]):
        ref[s, pl.ds(start, size), :] = val[:, s * LANES:(s + 1) * LANES]


def _slab_rows(ref, start, size):
    return jnp.concatenate([ref[s, pl.ds(start, size), :] for s in range(ref.shape[0])], axis=1)


def _fft_stage1(plan, load_rows, a_ref, w1_ref):
    w1 = w1_ref[...]
    g = F32_SUBLANE_TILE

    def body(c, carry):
        tiles = [load_rows(pl.multiple_of(blk * plan.n2 + c * g, g), g) for blk in range(plan.nb)]
        rhs = jnp.concatenate(tiles, axis=0).astype(BF16)
        a = jnp.dot(w1, rhs, preferred_element_type=F32)
        for k in range(plan.pa):
            _slab_store_block(a_ref, pl.multiple_of(k * plan.n2 + c * g, g), g, a[k * g:(k + 1) * g])
        return carry

    lax.fori_loop(0, plan.n2 // g, body, 0, unroll=8)


def _fft_stage2(plan, a_ref, w3_ref, k1):
    n2 = plan.n2
    aligned = (lambda r: r) if isinstance(k1, int) else (lambda r: pl.multiple_of(r, n2))
    re = _slab_rows(a_ref, aligned(k1 * n2), n2)
    im = _slab_rows(a_ref, aligned((plan.h1 + k1) * n2), n2)
    rhs = jnp.concatenate([re, im], axis=0).astype(BF16)
    return jnp.dot(w3_ref[k1], rhs, preferred_element_type=F32)


def _spectrum_kernel(plan, ff_ref, fb_ref, w1_ref, w3_ref, o_ref, af_ref, ab_ref):
    n2 = plan.n2
    _fft_stage1(plan, lambda start, size: ff_ref[pl.ds(start, size), :], af_ref, w1_ref)
    _fft_stage1(plan, lambda start, size: fb_ref[pl.ds(start, size), :], ab_ref, w1_ref)
    hb0 = fb_ref[0:1, :]

    def body(k1, carry):
        xf = _fft_stage2(plan, af_ref, w3_ref, k1)
        xb = _fft_stage2(plan, ab_ref, w3_ref, k1)
        o_ref[k1, 0:n2, :] = (xf[:n2] + xb[:n2] - hb0).astype(o_ref.dtype)
        o_ref[k1, n2:2 * n2, :] = (xf[n2:] - xb[n2:]).astype(o_ref.dtype)
        return carry

    lax.fori_loop(0, plan.h1, body, 0, unroll=11)


def _filter_spectrum(plan, filt, ct):
    seq_len, c2 = filt.shape
    c = c2 // 2
    assert c % ct == 0 and ct % LANES == 0
    nct = c // ct
    nslab = ct // LANES
    n2, h1, pa = plan.n2, plan.h1, plan.pa
    return pl.pallas_call(
        functools.partial(_spectrum_kernel, plan),
        grid=(nct,),
        in_specs=[
            pl.BlockSpec((seq_len, ct), lambda j: (0, j)),
            pl.BlockSpec((seq_len, ct), lambda j: (0, nct + j)),
            _const_spec(plan.w1.shape),
            _const_spec(plan.w3.shape),
        ],
        out_specs=pl.BlockSpec((h1, 2 * n2, ct), lambda j: (0, 0, j)),
        out_shape=jax.ShapeDtypeStruct((h1, 2 * n2, c), BF16),
        scratch_shapes=[
            pltpu.VMEM((nslab, n2 * pa, LANES), F32),
            pltpu.VMEM((nslab, n2 * pa, LANES), F32),
        ],
        compiler_params=_cparams(1),
        name="filter_spectrum",
    )(filt, filt, plan.w1, plan.w3)


def _short_conv_rows(p_ref, w_ref, b_ref, r0, rows, seq_len, stage_ref):
    halo = BF16_SUBLANE_TILE
    main = p_ref[pl.ds(r0, rows), :].astype(F32)
    prev_start = pl.multiple_of(jnp.maximum(r0 - halo, 0), halo)
    next_start = pl.multiple_of(jnp.minimum(r0 + rows, seq_len - halo), halo)
    prev_blk = p_ref[pl.ds(prev_start, halo), :].astype(F32)
    next_blk = p_ref[pl.ds(next_start, halo), :].astype(F32)
    pad = F32_SUBLANE_TILE
    prev_rows = jnp.where(r0 > 0, prev_blk[halo - pad:, :], 0.0)
    next_rows = jnp.where(r0 + rows < seq_len, next_blk[:pad, :], 0.0)
    _slab_store_block(stage_ref, 0, pad, prev_rows)
    _slab_store_block(stage_ref, pad, rows, main)
    _slab_store_block(stage_ref, pad + rows, pad, next_rows)
    up = _slab_rows(stage_ref, pad - 1, rows)
    dn = _slab_rows(stage_ref, pad + 1, rows)
    w = w_ref[...]
    return w[0:1] * up + w[1:2] * main + w[2:3] * dn + b_ref[...]


def _hyena_kernel(plan, x0_ref, x1_ref, v_ref, cw0_ref, cw1_ref, cw2_ref, cb0_ref, cb1_ref, cb2_ref,
                  skip_ref, ks_ref, w1_ref, w3_ref, w3t_ref, mi_ref, o_ref, u_ref, a_ref, b_ref,
                  stage_ref):
    seq_len, n2, nb, h1 = plan.seq_len, plan.n2, plan.nb, plan.h1
    rows = CONV_ROWS
    g = F32_SUBLANE_TILE

    def conv_u(c, carry):
        r0 = pl.multiple_of(c * rows, rows)
        x1c = _short_conv_rows(x1_ref, cw1_ref, cb1_ref, r0, rows, seq_len, stage_ref.at[0])
        vc = _short_conv_rows(v_ref, cw2_ref, cb2_ref, r0, rows, seq_len, stage_ref.at[1])
        _slab_store_block(u_ref, r0, rows, x1c * vc)
        return carry

    lax.fori_loop(0, seq_len // rows, conv_u, 0)

    _fft_stage1(plan, functools.partial(_slab_rows, u_ref), a_ref, w1_ref)

    for k1 in range(h1):
        x = _fft_stage2(plan, a_ref, w3_ref, k1)
        k = ks_ref[k1].astype(F32)
        xr, xi, kr, ki = x[:n2], x[n2:], k[:n2], k[n2:]
        y = jnp.concatenate([xr * kr - xi * ki, xr * ki + xi * kr], axis=0).astype(BF16)
        bm = jnp.dot(w3t_ref[k1], y, preferred_element_type=F32)
        _slab_store_block(b_ref, k1 * n2, n2, bm[:n2])
        im_block = plan.dump_block if k1 in (0, h1 - 1) else h1 - 1 + k1
        _slab_store_block(b_ref, im_block * n2, n2, bm[n2:])

    mi = mi_ref[...]
    skip = skip_ref[...]

    def last_stage(c, carry):
        tiles = [_slab_rows(b_ref, pl.multiple_of(q * n2 + c * g, g), g) for q in range(plan.nq)]
        rhs = jnp.concatenate(tiles, axis=0).astype(BF16)
        y = jnp.dot(mi, rhs, preferred_element_type=F32)
        for blk in range(nb):
            start = pl.multiple_of(blk * n2 + c * g, g)
            ucur = _slab_rows(u_ref, start, g)
            _slab_store_block(u_ref, start, g, y[blk * g:(blk + 1) * g] + skip * ucur)
        return carry

    lax.fori_loop(0, n2 // g, last_stage, 0, unroll=8)

    def gate_out(c, carry):
        r0 = pl.multiple_of(c * rows, rows)
        x0c = _short_conv_rows(x0_ref, cw0_ref, cb0_ref, r0, rows, seq_len, stage_ref.at[0])
        o_ref[pl.ds(r0, rows), :] = (x0c * _slab_rows(u_ref, r0, rows)).astype(o_ref.dtype)
        return carry

    lax.fori_loop(0, seq_len // rows, gate_out, 0)


def _hyena_operator(plan, proj3, conv_w, conv_b, skip, kspec, ct):
    bsz, seq_len, _ = proj3.shape
    c = skip.shape[-1]
    assert c % ct == 0 and ct % LANES == 0 and seq_len % CONV_ROWS == 0
    nct = c // ct
    nslab = ct // LANES
    n2, h1, pa = plan.n2, plan.h1, plan.pa
    cb = conv_b.reshape(1, -1)
    part = lambda k: pl.BlockSpec((None, seq_len, ct), lambda j, b, k=k: (b, 0, k * nct + j))
    wpart = lambda k, r: pl.BlockSpec((r, ct), lambda j, b, k=k: (0, k * nct + j))
    return pl.pallas_call(
        functools.partial(_hyena_kernel, plan),
        grid=(nct, bsz),
        in_specs=[
            part(0), part(1), part(2),
            wpart(0, 3), wpart(1, 3), wpart(2, 3),
            wpart(0, 1), wpart(1, 1), wpart(2, 1),
            pl.BlockSpec((1, ct), lambda j, b: (0, j)),
            pl.BlockSpec((h1, 2 * n2, ct), lambda j, b: (0, 0, j), pipeline_mode=pl.Buffered(1)),
            _const_spec(plan.w1.shape),
            _const_spec(plan.w3.shape),
            _const_spec(plan.w3t.shape),
            _const_spec(plan.mi.shape),
        ],
        out_specs=pl.BlockSpec((None, seq_len, ct), lambda j, b: (b, 0, j)),
        out_shape=jax.ShapeDtypeStruct((bsz, seq_len, c), BF16),
        scratch_shapes=[
            pltpu.VMEM((nslab, seq_len, LANES), F32),
            pltpu.VMEM((nslab, n2 * pa, LANES), F32),
            pltpu.VMEM((nslab, (plan.nq + 1) * n2, LANES), F32),
            pltpu.VMEM((2, nslab, CONV_ROWS + 2 * F32_SUBLANE_TILE, LANES), F32),
        ],
        compiler_params=_cparams(2),
        name="hyena_operator",
    )(proj3, proj3, proj3, conv_w, conv_w, conv_w, cb, cb, cb, skip.reshape(1, -1), kspec,
      plan.w1, plan.w3, plan.w3t, plan.mi)


def _merge_kernel(us_ref, vs_ref, ga_ref, gb_ref, ya_ref, x_ref, sg_ref, sw_ref, sbt_ref,
                  pa_ref, pb_ref, wo_ref, o_ref, yb_ref):
    heads, chunk, _ = sw_ref.shape
    tm, width = vs_ref.shape
    hd = width // heads
    v = vs_ref[...].astype(F32)
    ms = jnp.mean(v * v, axis=-1, keepdims=True)
    vn_all = (v * lax.rsqrt(ms + EPS) * sg_ref[...]).astype(BF16)
    for r in range(tm // chunk):
        rs = slice(r * chunk, (r + 1) * chunk)
        for g in range(heads):
            cs = slice(g * hd, (g + 1) * hd)
            s = (jnp.dot(sw_ref[g].astype(BF16), vn_all[rs, cs], preferred_element_type=F32)
                 + sbt_ref[:, g:g + 1])
            yb_ref[rs, cs] = (us_ref[rs, cs].astype(F32) * s).astype(BF16)
    pa = jnp.dot(ya_ref[...], pa_ref[...].astype(BF16), preferred_element_type=F32)
    pb = jnp.dot(yb_ref[...], pb_ref[...].astype(BF16), preferred_element_type=F32)
    merged = ga_ref[...].astype(F32) * pa + gb_ref[...].astype(F32) * pb
    o_ref[...] = x_ref[...] + jnp.dot(merged.astype(BF16), wo_ref[...].astype(BF16),
                                      preferred_element_type=F32)


def _merge(proj, ya, x2d, sgu_g, sgu_w, sgu_b, pa_w, pb_w, wo_w, hy_cols, tm):
    t, d = x2d.shape
    width = sgu_g.shape[-1]
    heads, chunk, _ = sgu_w.shape
    assert t % tm == 0 and tm % chunk == 0 and hy_cols % width == 0 and ya.shape[1] == width
    base = hy_cols // width
    col = lambda k: pl.BlockSpec((tm, width), lambda i, k=k: (i, base + k))
    sbt = sgu_b.T
    return pl.pallas_call(
        _merge_kernel,
        grid=(t // tm,),
        in_specs=[
            col(0), col(1), col(2), col(3),
            pl.BlockSpec((tm, width), lambda i: (i, 0)),
            pl.BlockSpec((tm, d), lambda i: (i, 0)),
            _const_spec((1, width)),
            _const_spec(sgu_w.shape),
            _const_spec(sbt.shape),
            _const_spec(pa_w.shape),
            _const_spec(pb_w.shape),
            _const_spec(wo_w.shape),
        ],
        out_specs=pl.BlockSpec((tm, d), lambda i: (i, 0)),
        out_shape=jax.ShapeDtypeStruct((t, d), F32),
        scratch_shapes=[pltpu.VMEM((tm, width), BF16)],
        compiler_params=_cparams(1),
        name="sgu_merge",
    )(proj, proj, proj, proj, ya, x2d, sgu_g.reshape(1, -1), sgu_w, sbt, pa_w, pb_w, wo_w)


FFN_HALO = 16
FFN_CHUNK = 256
FFN_SUB_ROWS = 512


def _ffn_kernel(xm_ref, xp_ref, xn_ref, g_ref, wup_ref, cw_ref, cb_ref, wdn_ref, fg_ref, o_ref,
                hs_ref, act_ref, *, tiles_per_seq, final_norm):
    tm, _ = xm_ref.shape
    hidden = wdn_ref.shape[0]
    halo = FFN_HALO
    i = pl.program_id(0)
    g = g_ref[...]

    def norm(xv):
        ms = jnp.mean(xv * xv, axis=-1, keepdims=True)
        return xv * lax.rsqrt(ms + EPS) * g

    at_start = (i % tiles_per_seq) == 0
    at_end = (i % tiles_per_seq) == tiles_per_seq - 1
    hs_ref[0:halo, :] = jnp.where(at_start, 0.0, norm(xp_ref[...])).astype(BF16)
    hs_ref[halo + tm:, :] = jnp.where(at_end, 0.0, norm(xn_ref[...])).astype(BF16)

    ts = FFN_SUB_ROWS
    ext = ts + 2 * halo
    for s in range(tm // ts):
        rs = slice(s * ts, (s + 1) * ts)
        hs_ref[halo + s * ts:halo + (s + 1) * ts, :] = norm(xm_ref[rs, :]).astype(BF16)
    for s in range(tm // ts):
        rs = slice(s * ts, (s + 1) * ts)
        h_ext = hs_ref[s * ts:s * ts + ext, :]
        h_main = hs_ref[halo + s * ts:halo + (s + 1) * ts, :]
        for k in range(hidden // FFN_CHUNK):
            cs = slice(k * FFN_CHUNK, (k + 1) * FFN_CHUNK)
            gs = slice(hidden + k * FFN_CHUNK, hidden + (k + 1) * FFN_CHUNK)
            a_ext = jnp.dot(h_ext, wup_ref[:, cs].astype(BF16), preferred_element_type=F32)
            gate = jnp.dot(h_main, wup_ref[:, gs].astype(BF16), preferred_element_type=F32)
            up = pltpu.roll(a_ext, 1, 0)[halo:halo + ts]
            dn = pltpu.roll(a_ext, ext - 1, 0)[halo:halo + ts]
            w = cw_ref[:, cs]
            c = w[0:1] * up + w[1:2] * a_ext[halo:halo + ts] + w[2:3] * dn + cb_ref[:, cs]
            act_ref[rs, cs] = (c / (1.0 + jnp.exp(-c)) * gate).astype(BF16)
        y = xm_ref[rs, :] + jnp.dot(act_ref[rs, :], wdn_ref[...].astype(BF16),
                                    preferred_element_type=F32)
        if final_norm:
            ms = jnp.mean(y * y, axis=-1, keepdims=True)
            y = y * lax.rsqrt(ms + EPS) * fg_ref[...]
        o_ref[rs, :] = y


def _ffn(x2d, seq_len, g, wup, conv_w, conv_b, wdn, final_g, final_norm, tm):
    t, d = x2d.shape
    hidden = wdn.shape[0]
    halo = FFN_HALO
    assert t % tm == 0 and seq_len % tm == 0 and tm % FFN_SUB_ROWS == 0 and FFN_SUB_ROWS % halo == 0
    assert hidden % FFN_CHUNK == 0
    per = tm // halo
    last = t // halo - 1
    kern = functools.partial(_ffn_kernel, tiles_per_seq=seq_len // tm, final_norm=final_norm)
    return pl.pallas_call(
        kern,
        grid=(t // tm,),
        in_specs=[
            pl.BlockSpec((tm, d), lambda i: (i, 0)),
            pl.BlockSpec((halo, d), lambda i: (jnp.maximum(i * per - 1, 0), 0)),
            pl.BlockSpec((halo, d), lambda i: (jnp.minimum((i + 1) * per, last), 0)),
            _const_spec((1, d)),
            _const_spec(wup.shape),
            _const_spec(conv_w.shape),
            _const_spec((1, hidden)),
            _const_spec(wdn.shape),
            _const_spec((1, d)),
        ],
        out_specs=pl.BlockSpec((tm, d), lambda i: (i, 0)),
        out_shape=jax.ShapeDtypeStruct((t, d), F32),
        scratch_shapes=[pltpu.VMEM((tm + 2 * halo, d), BF16), pltpu.VMEM((tm, hidden), BF16)],
        compiler_params=_cparams(1),
        name="ffn",
    )(x2d, x2d, x2d, g.reshape(1, -1), wup, conv_w, conv_b.reshape(1, -1), wdn,
      final_g.reshape(1, -1))


def kernel(x, norm1_g, w_in, hy_conv_w, hy_conv_b, filt_w1, filt_b1, filt_w2, filt_b2, filt_w3, filt_b3, filt_freq, filt_w4, hy_decay, hy_skip, sgu_norm_g, sgu_w, sgu_b, w_proj_hyena, w_proj_sgu, w_out, norm2_g, w_up, ffn_conv_w, ffn_conv_b, w_down, final_g):
    bsz, seq_len, d = x.shape
    depth = norm1_g.shape[0]
    hw = hy_skip.shape[-1]
    sw = sgu_norm_g.shape[-1]
    plan = _FftPlan(seq_len)
    x2d = x.reshape(bsz * seq_len, d)
    for i in range(depth):
        proj = _in_projection(x2d, norm1_g[i].reshape(1, -1), w_in[i],
                              n_raw_cols=3 * hw, n_gelu_cols=2 * sw, tm=512, tn=512)
        filt = _implicit_filters(seq_len, filt_w1[i], filt_b1[i], filt_w2[i], filt_b2[i], filt_w3[i],
                                 filt_b3[i], filt_freq[i], filt_w4[i], hy_decay[i])
        kspec = _filter_spectrum(plan, filt, HYENA_CH)
        ya = _hyena_operator(plan, proj.reshape(bsz, seq_len, -1), hy_conv_w[i], hy_conv_b[i],
                             hy_skip[i], kspec, HYENA_CH)
        x2d = _merge(proj, ya.reshape(bsz * seq_len, hw), x2d, sgu_norm_g[i], sgu_w[i],
                     sgu_b[i], w_proj_hyena[i], w_proj_sgu[i], w_out[i], hy_cols=3 * hw, tm=512)
        x2d = _ffn(x2d, seq_len, norm2_g[i], w_up[i], ffn_conv_w[i], ffn_conv_b[i],
                   w_down[i], final_g, final_norm=(i == depth - 1), tm=1024)
    return x2d.reshape(bsz, seq_len, d)
```

```python
import functools
import math

import numpy as np
import jax
import jax.numpy as jnp
from jax import lax
from jax.experimental import pallas as pl
from jax.experimental.pallas import tpu as pltpu

EPS = 1e-6
F32 = jnp.float32
BF16 = jnp.bfloat16

LANES = 128
F32_SUBLANE_TILE = 8
BF16_SUBLANE_TILE = 16
VMEM_LIMIT_BYTES = 60 * 1024 * 1024

FFT_INNER = 128
CONV_ROWS = 256
HYENA_CH = 256
MERGE_SUB_ROWS = 512


def _round_up(n, m):
    return -(-n // m) * m


def _cparams(n_axes):
    return pltpu.CompilerParams(
        dimension_semantics=("arbitrary",) * n_axes, vmem_limit_bytes=VMEM_LIMIT_BYTES)


def _const_spec(shape):
    nd = len(shape)
    return pl.BlockSpec(shape, lambda *_: (0,) * nd, pipeline_mode=pl.Buffered(1))


def _inproj_kernel(x_ref, g_ref, w_ref, o_ref, hn_ref, *, n_raw, n_gelu, tn):
    xv = x_ref[...]
    ms = jnp.mean(xv * xv, axis=-1, keepdims=True)
    hn_ref[...] = (xv * lax.rsqrt(ms + EPS) * g_ref[...]).astype(BF16)
    for j in range(o_ref.shape[1] // tn):
        cs = slice(j * tn, (j + 1) * tn)
        acc = jnp.dot(hn_ref[...], w_ref[:, cs].astype(BF16), preferred_element_type=F32)
        if j < n_raw:
            out = acc
        elif j < n_raw + n_gelu:
            out = 0.5 * acc * (1.0 + lax.erf(acc * math.sqrt(0.5)))
        else:
            out = 1.0 / (1.0 + jnp.exp(-acc))
        o_ref[:, cs] = out.astype(o_ref.dtype)


def _in_projection(x2d, g, w_in, n_raw_cols, n_gelu_cols, tm, tn):
    t, d = x2d.shape
    n = w_in.shape[1]
    assert t % tm == 0 and n % tn == 0 and n_raw_cols % tn == 0 and n_gelu_cols % tn == 0
    kern = functools.partial(_inproj_kernel, n_raw=n_raw_cols // tn, n_gelu=n_gelu_cols // tn, tn=tn)
    return pl.pallas_call(
        kern,
        grid=(t // tm,),
        in_specs=[
            pl.BlockSpec((tm, d), lambda i: (i, 0)),
            _const_spec((1, d)),
            _const_spec(w_in.shape),
        ],
        out_specs=pl.BlockSpec((tm, n), lambda i: (i, 0)),
        out_shape=jax.ShapeDtypeStruct((t, n), BF16),
        scratch_shapes=[pltpu.VMEM((tm, d), BF16)],
        compiler_params=_cparams(1),
        name="in_projection",
    )(x2d, g, w_in)


def _filter_kernel(w1t_ref, w1c_ref, w1s_ref, b1_ref, w2_ref, b2_ref, w3_ref, b3_ref, fr_ref,
                   w4_ref, dec_ref, o_ref, *, seq_len, bands):
    tl = o_ref.shape[0]
    hi = lax.Precision.HIGHEST
    dot = functools.partial(jnp.dot, preferred_element_type=F32, precision=hi)
    r0 = pl.program_id(0) * tl
    pos = (r0 + lax.broadcasted_iota(jnp.int32, (1, tl), 1)).astype(F32)
    t = pos / float(seq_len - 1)
    band_step = (bands - 1 - 1e-4) / (bands - 1)
    band = 1e-4 + band_step * lax.broadcasted_iota(jnp.int32, (bands, 1), 0).astype(F32)
    phase = ((2.0 * math.pi / seq_len) * pos) * band
    a = fr_ref[...]
    z1 = w1t_ref[...] * t + dot(w1c_ref[...], jnp.cos(phase)) - dot(w1s_ref[...], jnp.sin(phase))
    h = jnp.sin(a * (z1 + b1_ref[...]))
    h = jnp.sin(a * (dot(w2_ref[...], h) + b2_ref[...]))
    h = jnp.sin(a * (dot(w3_ref[...], h) + b3_ref[...]))
    def split(v):
        v_hi = v.astype(BF16)
        return v_hi, (v - v_hi.astype(F32)).astype(BF16)

    h_hi, h_lo = split(h)
    w_hi, w_lo = split(w4_ref[...])
    lhs = jnp.concatenate([h_hi, h_lo, h_hi, h_lo], axis=0)
    rhs = jnp.concatenate([w_hi, w_hi, w_lo, w_lo], axis=0)
    f = lax.dot_general(lhs, rhs, (((0,), (0,)), ((), ())), preferred_element_type=F32)
    t_col = (r0 + lax.broadcasted_iota(jnp.int32, (tl, 1), 0)).astype(F32) / float(seq_len - 1)
    o_ref[...] = f * jnp.exp(-t_col * jnp.abs(dec_ref[...]))


def _implicit_filters(seq_len, w1, b1, w2, b2, w3, b3, freq, w4, decay, tl=512):
    bands = (w1.shape[0] - 1) // 2
    hid = w1.shape[1]
    n_out = w4.shape[1]
    assert seq_len % tl == 0
    col = lambda v: v.reshape(-1, 1)
    args = (w1[0:1].T, w1[1:1 + bands].T, w1[1 + bands:].T, col(b1), w2.T, col(b2), w3.T, col(b3),
            col(freq), w4, decay.reshape(1, -1))
    full = lambda a: pl.BlockSpec(a.shape, lambda i: (0, 0))
    kern = functools.partial(_filter_kernel, seq_len=seq_len, bands=bands)
    del hid
    return pl.pallas_call(
        kern,
        grid=(seq_len // tl,),
        in_specs=[full(a) for a in args],
        out_specs=pl.BlockSpec((tl, n_out), lambda i: (i, 0)),
        out_shape=jax.ShapeDtypeStruct((seq_len, n_out), F32),
        compiler_params=_cparams(1),
        name="implicit_filter",
    )(*args)


class _FftPlan:
    def __init__(self, seq_len):
        n2 = FFT_INNER
        assert seq_len % n2 == 0
        self.seq_len = seq_len
        self.n = 2 * seq_len
        self.n2 = n2
        self.n1 = self.n // n2
        self.nb = seq_len // n2
        self.h1 = self.n1 // 2 + 1
        self.pa = 2 * self.h1
        self.nq = self.n1
        self.dump_block = self.n1
        assert self.nb % BF16_SUBLANE_TILE == 0
        n1, nb, h1, n = self.n1, self.nb, self.h1, self.n
        k1 = np.arange(h1)[:, None]
        ph = 2.0 * np.pi * k1 * np.arange(nb)[None, :] / n1
        w1 = np.zeros((self.pa, nb))
        w1[:h1] = np.cos(ph)
        w1[h1:2 * h1] = -np.sin(ph)
        k2 = np.arange(n2)[:, None]
        m2 = np.arange(n2)[None, :]
        w3 = np.zeros((h1, 2 * n2, 2 * n2))
        for k in range(h1):
            th = 2.0 * np.pi * (((k + n1 * k2) * m2) % n) / n
            er, ei = np.cos(th), -np.sin(th)
            w3[k] = np.block([[er, -ei], [ei, er]])
        ck = np.full(h1, 2.0)
        ck[0] = 1.0
        ck[-1] = 1.0
        phi = 2.0 * np.pi * np.arange(nb)[:, None] * np.arange(h1)[None, :] / n1
        mi = np.zeros((nb, self.nq))
        mi[:, :h1] = ck * np.cos(phi) / n
        mi[:, h1:] = (-ck * np.sin(phi) / n)[:, 1:h1 - 1]
        g = F32_SUBLANE_TILE
        w1k = np.einsum("kn,ij->kinj", w1, np.eye(g)).reshape(self.pa * g, nb * g)
        self.w1 = jnp.asarray(w1k, F32).astype(BF16)
        self.w3 = jnp.asarray(w3, F32).astype(BF16)
        self.w3t = jnp.asarray(np.transpose(w3, (0, 2, 1)), F32).astype(BF16)
        mik = np.einsum("nq,ij->niqj", mi, np.eye(g)).reshape(nb * g, self.nq * g)
        self.mi = jnp.asarray(mik, F32).astype(BF16)


def _slab_store_block(ref, start, size, val):
    for s in range(ref.shape[0]):
        ref[s, pl.ds(start, size), :] = val[:, s * LANES:(s + 1) * LANES]


def _slab_rows(ref, start, size):
    return jnp.concatenate([ref[s, pl.ds(start, size), :] for s in range(ref.shape[0])], axis=1)


def _fft_stage1(plan, load_rows, a_ref, w1_ref):
    w1 = w1_ref[...]
    g = F32_SUBLANE_TILE

    def body(c, carry):
        tiles = [load_rows(pl.multiple_of(blk * plan.n2 + c * g, g), g) for blk in range(plan.nb)]
        rhs = jnp.concatenate(tiles, axis=0).astype(BF16)
        a = jnp.dot(w1, rhs, preferred_element_type=F32)
        for k in range(plan.pa):
            _slab_store_block(a_ref, pl.multiple_of(k * plan.n2 + c * g, g), g, a[k * g:(k + 1) * g])
        return carry

    lax.fori_loop(0, plan.n2 // g, body, 0, unroll=8)


def _fft_stage2(plan, a_ref, w3_ref, k1):
    n2 = plan.n2
    aligned = (lambda r: r) if isinstance(k1, int) else (lambda r: pl.multiple_of(r, n2))
    re = _slab_rows(a_ref, aligned(k1 * n2), n2)
    im = _slab_rows(a_ref, aligned((plan.h1 + k1) * n2), n2)
    rhs = jnp.concatenate([re, im], axis=0).astype(BF16)
    return jnp.dot(w3_ref[k1], rhs, preferred_element_type=F32)


def _spectrum_kernel(plan, ff_ref, fb_ref, w1_ref, w3_ref, o_ref, af_ref, ab_ref):
    n2 = plan.n2
    _fft_stage1(plan, lambda start, size: ff_ref[pl.ds(start, size), :], af_ref, w1_ref)
    _fft_stage1(plan, lambda start, size: fb_ref[pl.ds(start, size), :], ab_ref, w1_ref)
    hb0 = fb_ref[0:1, :]

    def body(k1, carry):
        xf = _fft_stage2(plan, af_ref, w3_ref, k1)
        xb = _fft_stage2(plan, ab_ref, w3_ref, k1)
        o_ref[k1, 0:n2, :] = (xf[:n2] + xb[:n2] - hb0).astype(o_ref.dtype)
        o_ref[k1, n2:2 * n2, :] = (xf[n2:] - xb[n2:]).astype(o_ref.dtype)
        return carry

    lax.fori_loop(0, plan.h1, body, 0, unroll=11)


def _filter_spectrum(plan, filt, ct):
    seq_len, c2 = filt.shape
    c = c2 // 2
    assert c % ct == 0 and ct % LANES == 0
    nct = c // ct
    nslab = ct // LANES
    n2, h1, pa = plan.n2, plan.h1, plan.pa
    return pl.pallas_call(
        functools.partial(_spectrum_kernel, plan),
        grid=(nct,),
        in_specs=[
            pl.BlockSpec((seq_len, ct), lambda j: (0, j)),
            pl.BlockSpec((seq_len, ct), lambda j: (0, nct + j)),
            _const_spec(plan.w1.shape),
            _const_spec(plan.w3.shape),
        ],
        out_specs=pl.BlockSpec((h1, 2 * n2, ct), lambda j: (0, 0, j)),
        out_shape=jax.ShapeDtypeStruct((h1, 2 * n2, c), BF16),
        scratch_shapes=[
            pltpu.VMEM((nslab, n2 * pa, LANES), F32),
            pltpu.VMEM((nslab, n2 * pa, LANES), F32),
        ],
        compiler_params=_cparams(1),
        name="filter_spectrum",
    )(filt, filt, plan.w1, plan.w3)


def _short_conv_rows(p_ref, w_ref, b_ref, r0, rows, seq_len, stage_ref):
    halo = BF16_SUBLANE_TILE
    main = p_ref[pl.ds(r0, rows), :].astype(F32)
    prev_start = pl.multiple_of(jnp.maximum(r0 - halo, 0), halo)
    next_start = pl.multiple_of(jnp.minimum(r0 + rows, seq_len - halo), halo)
    prev_blk = p_ref[pl.ds(prev_start, halo), :].astype(F32)
    next_blk = p_ref[pl.ds(next_start, halo), :].astype(F32)
    pad = F32_SUBLANE_TILE
    prev_rows = jnp.where(r0 > 0, prev_blk[halo - pad:, :], 0.0)
    next_rows = jnp.where(r0 + rows < seq_len, next_blk[:pad, :], 0.0)
    _slab_store_block(stage_ref, 0, pad, prev_rows)
    _slab_store_block(stage_ref, pad, rows, main)
    _slab_store_block(stage_ref, pad + rows, pad, next_rows)
    up = _slab_rows(stage_ref, pad - 1, rows)
    dn = _slab_rows(stage_ref, pad + 1, rows)
    w = w_ref[...]
    return w[0:1] * up + w[1:2] * main + w[2:3] * dn + b_ref[...]


def _hyena_kernel(plan, x0_ref, x1_ref, v_ref, cw0_ref, cw1_ref, cw2_ref, cb0_ref, cb1_ref, cb2_ref,
                  skip_ref, ks_ref, w1_ref, w3_ref, w3t_ref, mi_ref, o_ref, u_ref, a_ref, b_ref,
                  stage_ref):
    seq_len, n2, nb, h1 = plan.seq_len, plan.n2, plan.nb, plan.h1
    rows = CONV_ROWS
    g = F32_SUBLANE_TILE

    def conv_u(c, carry):
        r0 = pl.multiple_of(c * rows, rows)
        x1c = _short_conv_rows(x1_ref, cw1_ref, cb1_ref, r0, rows, seq_len, stage_ref.at[0])
        vc = _short_conv_rows(v_ref, cw2_ref, cb2_ref, r0, rows, seq_len, stage_ref.at[1])
        _slab_store_block(u_ref, r0, rows, x1c * vc)
        return carry

    lax.fori_loop(0, seq_len // rows, conv_u, 0)

    _fft_stage1(plan, functools.partial(_slab_rows, u_ref), a_ref, w1_ref)

    for k1 in range(h1):
        x = _fft_stage2(plan, a_ref, w3_ref, k1)
        k = ks_ref[k1].astype(F32)
        xr, xi, kr, ki = x[:n2], x[n2:], k[:n2], k[n2:]
        y = jnp.concatenate([xr * kr - xi * ki, xr * ki + xi * kr], axis=0).astype(BF16)
        bm = jnp.dot(w3t_ref[k1], y, preferred_element_type=F32)
        _slab_store_block(b_ref, k1 * n2, n2, bm[:n2])
        im_block = plan.dump_block if k1 in (0, h1 - 1) else h1 - 1 + k1
        _slab_store_block(b_ref, im_block * n2, n2, bm[n2:])

    mi = mi_ref[...]
    skip = skip_ref[...]

    def last_stage(c, carry):
        tiles = [_slab_rows(b_ref, pl.multiple_of(q * n2 + c * g, g), g) for q in range(plan.nq)]
        rhs = jnp.concatenate(tiles, axis=0).astype(BF16)
        y = jnp.dot(mi, rhs, preferred_element_type=F32)
        for blk in range(nb):
            start = pl.multiple_of(blk * n2 + c * g, g)
            ucur = _slab_rows(u_ref, start, g)
            _slab_store_block(u_ref, start, g, y[blk * g:(blk + 1) * g] + skip * ucur)
        return carry

    lax.fori_loop(0, n2 // g, last_stage, 0, unroll=8)

    def gate_out(c, carry):
        r0 = pl.multiple_of(c * rows, rows)
        x0c = _short_conv_rows(x0_ref, cw0_ref, cb0_ref, r0, rows, seq_len, stage_ref.at[0])
        o_ref[pl.ds(r0, rows), :] = (x0c * _slab_rows(u_ref, r0, rows)).astype(o_ref.dtype)
        return carry

    lax.fori_loop(0, seq_len // rows, gate_out, 0)


def _hyena_operator(plan, proj3, conv_w, conv_b, skip, kspec, ct):
    bsz, seq_len, _ = proj3.shape
    c = skip.shape[-1]
    assert c % ct == 0 and ct % LANES == 0 and seq_len % CONV_ROWS == 0
    nct = c // ct
    nslab = ct // LANES
    n2, h1, pa = plan.n2, plan.h1, plan.pa
    cb = conv_b.reshape(1, -1)
    part = lambda k: pl.BlockSpec((None, seq_len, ct), lambda j, b, k=k: (b, 0, k * nct + j))
    wpart = lambda k, r: pl.BlockSpec((r, ct), lambda j, b, k=k: (0, k * nct + j))
    return pl.pallas_call(
        functools.partial(_hyena_kernel, plan),
        grid=(nct, bsz),
        in_specs=[
            part(0), part(1), part(2),
            wpart(0, 3), wpart(1, 3), wpart(2, 3),
            wpart(0, 1), wpart(1, 1), wpart(2, 1),
            pl.BlockSpec((1, ct), lambda j, b: (0, j)),
            pl.BlockSpec((h1, 2 * n2, ct), lambda j, b: (0, 0, j)),
            _const_spec(plan.w1.shape),
            _const_spec(plan.w3.shape),
            _const_spec(plan.w3t.shape),
            _const_spec(plan.mi.shape),
        ],
        out_specs=pl.BlockSpec((None, seq_len, ct), lambda j, b: (b, 0, j)),
        out_shape=jax.ShapeDtypeStruct((bsz, seq_len, c), BF16),
        scratch_shapes=[
            pltpu.VMEM((nslab, seq_len, LANES), F32),
            pltpu.VMEM((nslab, n2 * pa, LANES), F32),
            pltpu.VMEM((nslab, (plan.nq + 1) * n2, LANES), F32),
            pltpu.VMEM((2, nslab, CONV_ROWS + 2 * F32_SUBLANE_TILE, LANES), F32),
        ],
        compiler_params=_cparams(2),
        name="hyena_operator",
    )(proj3, proj3, proj3, conv_w, conv_w, conv_w, cb, cb, cb, skip.reshape(1, -1), kspec,
      plan.w1, plan.w3, plan.w3t, plan.mi)


def _merge_kernel(us_ref, vs_ref, ga_ref, gb_ref, ya_ref, x_ref, sg_ref, sw_ref, sbt_ref,
                  pa_ref, pb_ref, wo_ref, o_ref, yb_ref):
    heads, chunk, _ = sw_ref.shape
    tm, width = vs_ref.shape
    hd = width // heads
    ts = MERGE_SUB_ROWS
    for t0 in range(0, tm, ts):
        rows = slice(t0, t0 + ts)
        v = vs_ref[rows, :].astype(F32)
        ms = jnp.mean(v * v, axis=-1, keepdims=True)
        vn_all = (v * lax.rsqrt(ms + EPS) * sg_ref[...]).astype(BF16)
        for r in range(ts // chunk):
            rs = slice(t0 + r * chunk, t0 + (r + 1) * chunk)
            for g in range(heads):
                cs = slice(g * hd, (g + 1) * hd)
                s = (jnp.dot(sw_ref[g].astype(BF16), vn_all[r * chunk:(r + 1) * chunk, cs],
                             preferred_element_type=F32) + sbt_ref[:, g:g + 1])
                yb_ref[rs, cs] = (us_ref[rs, cs].astype(F32) * s).astype(BF16)
        pa = jnp.dot(ya_ref[rows, :], pa_ref[...].astype(BF16), preferred_element_type=F32)
        pb = jnp.dot(yb_ref[rows, :], pb_ref[...].astype(BF16), preferred_element_type=F32)
        merged = ga_ref[rows, :].astype(F32) * pa + gb_ref[rows, :].astype(F32) * pb
        o_ref[rows, :] = x_ref[rows, :] + jnp.dot(merged.astype(BF16), wo_ref[...].astype(BF16),
                                                  preferred_element_type=F32)


def _merge(proj, ya, x2d, sgu_g, sgu_w, sgu_b, pa_w, pb_w, wo_w, hy_cols, tm):
    t, d = x2d.shape
    width = sgu_g.shape[-1]
    heads, chunk, _ = sgu_w.shape
    assert t % tm == 0 and tm % MERGE_SUB_ROWS == 0 and MERGE_SUB_ROWS % chunk == 0
    assert hy_cols % width == 0 and ya.shape[1] == width
    base = hy_cols // width
    col = lambda k: pl.BlockSpec((tm, width), lambda i, k=k: (i, base + k))
    sbt = sgu_b.T
    return pl.pallas_call(
        _merge_kernel,
        grid=(t // tm,),
        in_specs=[
            col(0), col(1), col(2), col(3),
            pl.BlockSpec((tm, width), lambda i: (i, 0)),
            pl.BlockSpec((tm, d), lambda i: (i, 0)),
            _const_spec((1, width)),
            _const_spec(sgu_w.shape),
            _const_spec(sbt.shape),
            _const_spec(pa_w.shape),
            _const_spec(pb_w.shape),
            _const_spec(wo_w.shape),
        ],
        out_specs=pl.BlockSpec((tm, d), lambda i: (i, 0)),
        out_shape=jax.ShapeDtypeStruct((t, d), F32),
        scratch_shapes=[pltpu.VMEM((tm, width), BF16)],
        compiler_params=_cparams(1),
        name="sgu_merge",
    )(proj, proj, proj, proj, ya, x2d, sgu_g.reshape(1, -1), sgu_w, sbt, pa_w, pb_w, wo_w)


FFN_HALO = 16
FFN_CHUNK = 256
FFN_SUB_ROWS = 512


def _ffn_kernel(xm_ref, xp_ref, xn_ref, g_ref, wup_ref, cw_ref, cb_ref, wdn_ref, fg_ref, o_ref,
                hs_ref, act_ref, *, tiles_per_seq, final_norm):
    tm, _ = xm_ref.shape
    hidden = wdn_ref.shape[0]
    halo = FFN_HALO
    i = pl.program_id(0)
    g = g_ref[...]

    def norm(xv):
        ms = jnp.mean(xv * xv, axis=-1, keepdims=True)
        return xv * lax.rsqrt(ms + EPS) * g

    at_start = (i % tiles_per_seq) == 0
    at_end = (i % tiles_per_seq) == tiles_per_seq - 1
    hs_ref[0:halo, :] = jnp.where(at_start, 0.0, norm(xp_ref[...])).astype(BF16)
    hs_ref[halo + tm:, :] = jnp.where(at_end, 0.0, norm(xn_ref[...])).astype(BF16)

    ts = FFN_SUB_ROWS
    ext = ts + 2 * halo
    for s in range(tm // ts):
        rs = slice(s * ts, (s + 1) * ts)
        hs_ref[halo + s * ts:halo + (s + 1) * ts, :] = norm(xm_ref[rs, :]).astype(BF16)
    for s in range(tm // ts):
        rs = slice(s * ts, (s + 1) * ts)
        h_ext = hs_ref[s * ts:s * ts + ext, :]
        h_main = hs_ref[halo + s * ts:halo + (s + 1) * ts, :]
        for k in range(hidden // FFN_CHUNK):
            cs = slice(k * FFN_CHUNK, (k + 1) * FFN_CHUNK)
            gs = slice(hidden + k * FFN_CHUNK, hidden + (k + 1) * FFN_CHUNK)
            a_ext = jnp.dot(h_ext, wup_ref[:, cs].astype(BF16), preferred_element_type=F32)
            gate = jnp.dot(h_main, wup_ref[:, gs].astype(BF16), preferred_element_type=F32)
            up = pltpu.roll(a_ext, 1, 0)[halo:halo + ts]
            dn = pltpu.roll(a_ext, ext - 1, 0)[halo:halo + ts]
            w = cw_ref[:, cs]
            c = w[0:1] * up + w[1:2] * a_ext[halo:halo + ts] + w[2:3] * dn + cb_ref[:, cs]
            act_ref[rs, cs] = (c / (1.0 + jnp.exp(-c)) * gate).astype(BF16)
        y = xm_ref[rs, :] + jnp.dot(act_ref[rs, :], wdn_ref[...].astype(BF16),
                                    preferred_element_type=F32)
        if final_norm:
            ms = jnp.mean(y * y, axis=-1, keepdims=True)
            y = y * lax.rsqrt(ms + EPS) * fg_ref[...]
        o_ref[rs, :] = y


def _ffn(x2d, seq_len, g, wup, conv_w, conv_b, wdn, final_g, final_norm, tm):
    t, d = x2d.shape
    hidden = wdn.shape[0]
    halo = FFN_HALO
    assert t % tm == 0 and seq_len % tm == 0 and tm % FFN_SUB_ROWS == 0 and FFN_SUB_ROWS % halo == 0
    assert hidden % FFN_CHUNK == 0
    per = tm // halo
    last = t // halo - 1
    kern = functools.partial(_ffn_kernel, tiles_per_seq=seq_len // tm, final_norm=final_norm)
    return pl.pallas_call(
        kern,
        grid=(t // tm,),
        in_specs=[
            pl.BlockSpec((tm, d), lambda i: (i, 0)),
            pl.BlockSpec((halo, d), lambda i: (jnp.maximum(i * per - 1, 0), 0)),
            pl.BlockSpec((halo, d), lambda i: (jnp.minimum((i + 1) * per, last), 0)),
            _const_spec((1, d)),
            _const_spec(wup.shape),
            _const_spec(conv_w.shape),
            _const_spec((1, hidden)),
            _const_spec(wdn.shape),
            _const_spec((1, d)),
        ],
        out_specs=pl.BlockSpec((tm, d), lambda i: (i, 0)),
        out_shape=jax.ShapeDtypeStruct((t, d), F32),
        scratch_shapes=[pltpu.VMEM((tm + 2 * halo, d), BF16), pltpu.VMEM((tm, hidden), BF16)],
        compiler_params=_cparams(1),
        name="ffn",
    )(x2d, x2d, x2d, g.reshape(1, -1), wup, conv_w, conv_b.reshape(1, -1), wdn,
      final_g.reshape(1, -1))


def kernel(x, norm1_g, w_in, hy_conv_w, hy_conv_b, filt_w1, filt_b1, filt_w2, filt_b2, filt_w3, filt_b3, filt_freq, filt_w4, hy_decay, hy_skip, sgu_norm_g, sgu_w, sgu_b, w_proj_hyena, w_proj_sgu, w_out, norm2_g, w_up, ffn_conv_w, ffn_conv_b, w_down, final_g):
    bsz, seq_len, d = x.shape
    depth = norm1_g.shape[0]
    hw = hy_skip.shape[-1]
    sw = sgu_norm_g.shape[-1]
    plan = _FftPlan(seq_len)
    x2d = x.reshape(bsz * seq_len, d)
    for i in range(depth):
        proj = _in_projection(x2d, norm1_g[i].reshape(1, -1), w_in[i],
                              n_raw_cols=3 * hw, n_gelu_cols=2 * sw, tm=512, tn=512)
        filt = _implicit_filters(seq_len, filt_w1[i], filt_b1[i], filt_w2[i], filt_b2[i], filt_w3[i],
                                 filt_b3[i], filt_freq[i], filt_w4[i], hy_decay[i])
        kspec = _filter_spectrum(plan, filt, HYENA_CH)
        ya = _hyena_operator(plan, proj.reshape(bsz, seq_len, -1), hy_conv_w[i], hy_conv_b[i],
                             hy_skip[i], kspec, HYENA_CH)
        x2d = _merge(proj, ya.reshape(bsz * seq_len, hw), x2d, sgu_norm_g[i], sgu_w[i],
                     sgu_b[i], w_proj_hyena[i], w_proj_sgu[i], w_out[i], hy_cols=3 * hw, tm=1024)
        x2d = _ffn(x2d, seq_len, norm2_g[i], w_up[i], ffn_conv_w[i], ffn_conv_b[i],
                   w_down[i], final_g, final_norm=(i == depth - 1), tm=1024)
    return x2d.reshape(bsz, seq_len, d)
```

```python
import functools
import math

import numpy as np
import jax
import jax.numpy as jnp
from jax import lax
from jax.experimental import pallas as pl
from jax.experimental.pallas import tpu as pltpu

EPS = 1e-6
F32 = jnp.float32
BF16 = jnp.bfloat16

LANES = 128
F32_SUBLANE_TILE = 8
BF16_SUBLANE_TILE = 16
VMEM_LIMIT_BYTES = 60 * 1024 * 1024

ROW_HALO = BF16_SUBLANE_TILE
FFT_INNER = 128
COPY_ROWS = 256
HYENA_CH = 256
MERGE_SUB_ROWS = 512


def _round_up(n, m):
    return -(-n // m) * m


def _cparams(n_axes):
    return pltpu.CompilerParams(
        dimension_semantics=("arbitrary",) * n_axes, vmem_limit_bytes=VMEM_LIMIT_BYTES)


def _const_spec(shape):
    nd = len(shape)
    return pl.BlockSpec(shape, lambda *_: (0,) * nd, pipeline_mode=pl.Buffered(1))


def _inproj_kernel(xm_ref, xp_ref, xn_ref, g_ref, w_ref, cw_ref, cb_ref, o_ref, hn_ref, stage_ref, *,
                   hw, n_gelu_cols, tn, tiles_per_seq):
    tm = xm_ref.shape[0]
    halo = ROW_HALO
    ext = tm + 2 * halo
    i = pl.program_id(0)
    g = g_ref[...]

    def norm(xv):
        ms = jnp.mean(xv * xv, axis=-1, keepdims=True)
        return xv * lax.rsqrt(ms + EPS) * g

    at_start = (i % tiles_per_seq) == 0
    at_end = (i % tiles_per_seq) == tiles_per_seq - 1
    hn_ref[0:halo, :] = jnp.where(at_start, 0.0, norm(xp_ref[...])).astype(BF16)
    hn_ref[halo:halo + tm, :] = norm(xm_ref[...]).astype(BF16)
    hn_ref[halo + tm:, :] = jnp.where(at_end, 0.0, norm(xn_ref[...])).astype(BF16)

    def conv_cols(col0, buf):
        cs = slice(col0, col0 + tn)
        acc = jnp.dot(hn_ref[...], w_ref[:, cs].astype(BF16), preferred_element_type=F32)
        stage = stage_ref.at[buf]
        _slab_store_block(stage, 0, ext, acc)
        up = _slab_rows(stage, halo - 1, tm)
        dn = _slab_rows(stage, halo + 1, tm)
        w = cw_ref[:, cs]
        return w[0:1] * up + w[1:2] * acc[halo:halo + tm] + w[2:3] * dn + cb_ref[:, cs]

    for k in range(hw // tn):
        o_ref[:, k * tn:(k + 1) * tn] = conv_cols(k * tn, 0).astype(o_ref.dtype)
    for k in range(hw // tn):
        u = conv_cols(hw + k * tn, 0) * conv_cols(2 * hw + k * tn, 1)
        o_ref[:, hw + k * tn:hw + (k + 1) * tn] = u.astype(o_ref.dtype)
    for k in range((w_ref.shape[1] - 3 * hw) // tn):
        cs = slice(3 * hw + k * tn, 3 * hw + (k + 1) * tn)
        acc = jnp.dot(hn_ref[halo:halo + tm, :], w_ref[:, cs].astype(BF16), preferred_element_type=F32)
        if k * tn < n_gelu_cols:
            out = 0.5 * acc * (1.0 + lax.erf(acc * math.sqrt(0.5)))
        else:
            out = 1.0 / (1.0 + jnp.exp(-acc))
        o_ref[:, 2 * hw + k * tn:2 * hw + (k + 1) * tn] = out.astype(o_ref.dtype)


def _in_projection(x2d, seq_len, g, w_in, conv_w, conv_b, hw, n_gelu_cols, tm, tn):
    t, d = x2d.shape
    n = w_in.shape[1]
    halo = ROW_HALO
    assert t % tm == 0 and seq_len % tm == 0 and tm % halo == 0 and tn % LANES == 0
    assert hw % tn == 0 and n_gelu_cols % tn == 0 and (n - 3 * hw) % tn == 0
    per = tm // halo
    last = t // halo - 1
    kern = functools.partial(_inproj_kernel, hw=hw, n_gelu_cols=n_gelu_cols, tn=tn,
                             tiles_per_seq=seq_len // tm)
    return pl.pallas_call(
        kern,
        grid=(t // tm,),
        in_specs=[
            pl.BlockSpec((tm, d), lambda i: (i, 0)),
            pl.BlockSpec((halo, d), lambda i: (jnp.maximum(i * per - 1, 0), 0)),
            pl.BlockSpec((halo, d), lambda i: (jnp.minimum((i + 1) * per, last), 0)),
            _const_spec((1, d)),
            _const_spec(w_in.shape),
            _const_spec(conv_w.shape),
            _const_spec((1, conv_b.shape[-1])),
        ],
        out_specs=pl.BlockSpec((tm, n - hw), lambda i: (i, 0)),
        out_shape=jax.ShapeDtypeStruct((t, n - hw), BF16),
        scratch_shapes=[pltpu.VMEM((tm + 2 * halo, d), BF16),
                        pltpu.VMEM((2, tn // LANES, tm + 2 * halo, LANES), F32)],
        compiler_params=_cparams(1),
        name="in_projection",
    )(x2d, x2d, x2d, g, w_in, conv_w, conv_b.reshape(1, -1))


def _filter_kernel(w1t_ref, w1c_ref, w1s_ref, b1_ref, w2_ref, b2_ref, w3_ref, b3_ref, fr_ref,
                   w4_ref, dec_ref, o_ref, *, seq_len, bands):
    tl = o_ref.shape[0]
    hi = lax.Precision.HIGHEST
    dot = functools.partial(jnp.dot, preferred_element_type=F32, precision=hi)
    r0 = pl.program_id(0) * tl
    pos = (r0 + lax.broadcasted_iota(jnp.int32, (1, tl), 1)).astype(F32)
    t = pos / float(seq_len - 1)
    band_step = (bands - 1 - 1e-4) / (bands - 1)
    band = 1e-4 + band_step * lax.broadcasted_iota(jnp.int32, (bands, 1), 0).astype(F32)
    phase = ((2.0 * math.pi / seq_len) * pos) * band
    a = fr_ref[...]
    z1 = w1t_ref[...] * t + dot(w1c_ref[...], jnp.cos(phase)) - dot(w1s_ref[...], jnp.sin(phase))
    h = jnp.sin(a * (z1 + b1_ref[...]))
    h = jnp.sin(a * (dot(w2_ref[...], h) + b2_ref[...]))
    h = jnp.sin(a * (dot(w3_ref[...], h) + b3_ref[...]))
    def split(v):
        v_hi = v.astype(BF16)
        return v_hi, (v - v_hi.astype(F32)).astype(BF16)

    h_hi, h_lo = split(h)
    w_hi, w_lo = split(w4_ref[...])
    lhs = jnp.concatenate([h_hi, h_lo, h_hi, h_lo], axis=0)
    rhs = jnp.concatenate([w_hi, w_hi, w_lo, w_lo], axis=0)
    f = lax.dot_general(lhs, rhs, (((0,), (0,)), ((), ())), preferred_element_type=F32)
    t_col = (r0 + lax.broadcasted_iota(jnp.int32, (tl, 1), 0)).astype(F32) / float(seq_len - 1)
    o_ref[...] = f * jnp.exp(-t_col * jnp.abs(dec_ref[...]))


def _implicit_filters(seq_len, w1, b1, w2, b2, w3, b3, freq, w4, decay, tl=512):
    bands = (w1.shape[0] - 1) // 2
    hid = w1.shape[1]
    n_out = w4.shape[1]
    assert seq_len % tl == 0
    col = lambda v: v.reshape(-1, 1)
    args = (w1[0:1].T, w1[1:1 + bands].T, w1[1 + bands:].T, col(b1), w2.T, col(b2), w3.T, col(b3),
            col(freq), w4, decay.reshape(1, -1))
    full = lambda a: pl.BlockSpec(a.shape, lambda i: (0, 0))
    kern = functools.partial(_filter_kernel, seq_len=seq_len, bands=bands)
    del hid
    return pl.pallas_call(
        kern,
        grid=(seq_len // tl,),
        in_specs=[full(a) for a in args],
        out_specs=pl.BlockSpec((tl, n_out), lambda i: (i, 0)),
        out_shape=jax.ShapeDtypeStruct((seq_len, n_out), F32),
        compiler_params=_cparams(1),
        name="implicit_filter",
    )(*args)


class _FftPlan:
    def __init__(self, seq_len):
        n2 = FFT_INNER
        assert seq_len % n2 == 0
        self.seq_len = seq_len
        self.n = 2 * seq_len
        self.n2 = n2
        self.n1 = self.n // n2
        self.nb = seq_len // n2
        self.h1 = self.n1 // 2 + 1
        self.pa = 2 * self.h1
        self.nq = self.n1
        self.dump_block = self.n1
        assert self.nb % BF16_SUBLANE_TILE == 0
        n1, nb, h1, n = self.n1, self.nb, self.h1, self.n
        k1 = np.arange(h1)[:, None]
        ph = 2.0 * np.pi * k1 * np.arange(nb)[None, :] / n1
        w1 = np.zeros((self.pa, nb))
        w1[:h1] = np.cos(ph)
        w1[h1:2 * h1] = -np.sin(ph)
        k2 = np.arange(n2)[:, None]
        m2 = np.arange(n2)[None, :]
        w3 = np.zeros((h1, 2 * n2, 2 * n2))
        for k in range(h1):
            th = 2.0 * np.pi * (((k + n1 * k2) * m2) % n) / n
            er, ei = np.cos(th), -np.sin(th)
            w3[k] = np.block([[er, -ei], [ei, er]])
        ck = np.full(h1, 2.0)
        ck[0] = 1.0
        ck[-1] = 1.0
        phi = 2.0 * np.pi * np.arange(nb)[:, None] * np.arange(h1)[None, :] / n1
        mi = np.zeros((nb, self.nq))
        mi[:, :h1] = ck * np.cos(phi) / n
        mi[:, h1:] = (-ck * np.sin(phi) / n)[:, 1:h1 - 1]
        g = F32_SUBLANE_TILE
        w1k = np.einsum("kn,ij->kinj", w1, np.eye(g)).reshape(self.pa * g, nb * g)
        self.w1 = jnp.asarray(w1k, F32).astype(BF16)
        self.w3 = jnp.asarray(w3, F32).astype(BF16)
        self.w3t = jnp.asarray(np.transpose(w3, (0, 2, 1)), F32).astype(BF16)
        mik = np.einsum("nq,ij->niqj", mi, np.eye(g)).reshape(nb * g, self.nq * g)
        self.mi = jnp.asarray(mik, F32).astype(BF16)


def _slab_store_block(ref, start, size, val):
    for s in range(ref.shape[0]):
        ref[s, pl.ds(start, size), :] = val[:, s * LANES:(s + 1) * LANES]


def _slab_rows(ref, start, size):
    return jnp.concatenate([ref[s, pl.ds(start, size), :] for s in range(ref.shape[0])], axis=1)


def _fft_stage1(plan, load_rows, a_ref, w1_ref):
    w1 = w1_ref[...]
    g = F32_SUBLANE_TILE

    def body(c, carry):
        tiles = [load_rows(pl.multiple_of(blk * plan.n2 + c * g, g), g) for blk in range(plan.nb)]
        rhs = jnp.concatenate(tiles, axis=0).astype(BF16)
        a = jnp.dot(w1, rhs, preferred_element_type=F32)
        for k in range(plan.pa):
            _slab_store_block(a_ref, pl.multiple_of(k * plan.n2 + c * g, g), g, a[k * g:(k + 1) * g])
        return carry

    lax.fori_loop(0, plan.n2 // g, body, 0, unroll=8)


def _fft_stage2(plan, a_ref, w3_ref, k1):
    n2 = plan.n2
    aligned = (lambda r: r) if isinstance(k1, int) else (lambda r: pl.multiple_of(r, n2))
    re = _slab_rows(a_ref, aligned(k1 * n2), n2)
    im = _slab_rows(a_ref, aligned((plan.h1 + k1) * n2), n2)
    rhs = jnp.concatenate([re, im], axis=0).astype(BF16)
    return jnp.dot(w3_ref[k1], rhs, preferred_element_type=F32)


def _spectrum_kernel(plan, ff_ref, fb_ref, w1_ref, w3_ref, o_ref, af_ref, ab_ref):
    n2 = plan.n2
    _fft_stage1(plan, lambda start, size: ff_ref[pl.ds(start, size), :], af_ref, w1_ref)
    _fft_stage1(plan, lambda start, size: fb_ref[pl.ds(start, size), :], ab_ref, w1_ref)
    hb0 = fb_ref[0:1, :]

    def body(k1, carry):
        xf = _fft_stage2(plan, af_ref, w3_ref, k1)
        xb = _fft_stage2(plan, ab_ref, w3_ref, k1)
        o_ref[k1, 0:n2, :] = (xf[:n2] + xb[:n2] - hb0).astype(o_ref.dtype)
        o_ref[k1, n2:2 * n2, :] = (xf[n2:] - xb[n2:]).astype(o_ref.dtype)
        return carry

    lax.fori_loop(0, plan.h1, body, 0, unroll=11)


def _filter_spectrum(plan, filt, ct):
    seq_len, c2 = filt.shape
    c = c2 // 2
    assert c % ct == 0 and ct % LANES == 0
    nct = c // ct
    nslab = ct // LANES
    n2, h1, pa = plan.n2, plan.h1, plan.pa
    return pl.pallas_call(
        functools.partial(_spectrum_kernel, plan),
        grid=(nct,),
        in_specs=[
            pl.BlockSpec((seq_len, ct), lambda j: (0, j)),
            pl.BlockSpec((seq_len, ct), lambda j: (0, nct + j)),
            _const_spec(plan.w1.shape),
            _const_spec(plan.w3.shape),
        ],
        out_specs=pl.BlockSpec((h1, 2 * n2, ct), lambda j: (0, 0, j)),
        out_shape=jax.ShapeDtypeStruct((h1, 2 * n2, c), BF16),
        scratch_shapes=[
            pltpu.VMEM((nslab, n2 * pa, LANES), F32),
            pltpu.VMEM((nslab, n2 * pa, LANES), F32),
        ],
        compiler_params=_cparams(1),
        name="filter_spectrum",
    )(filt, filt, plan.w1, plan.w3)


def _hyena_kernel(plan, x0_ref, uin_ref, skip_ref, ks_ref, w1_ref, w3_ref, w3t_ref, mi_ref, o_ref,
                  u_ref, a_ref, b_ref):
    seq_len, n2, nb, h1 = plan.seq_len, plan.n2, plan.nb, plan.h1
    rows = COPY_ROWS
    g = F32_SUBLANE_TILE

    def load_u(c, carry):
        r0 = pl.multiple_of(c * rows, rows)
        _slab_store_block(u_ref, r0, rows, uin_ref[pl.ds(r0, rows), :].astype(F32))
        return carry

    lax.fori_loop(0, seq_len // rows, load_u, 0)

    _fft_stage1(plan, functools.partial(_slab_rows, u_ref), a_ref, w1_ref)

    for k1 in range(h1):
        x = _fft_stage2(plan, a_ref, w3_ref, k1)
        k = ks_ref[k1].astype(F32)
        xr, xi, kr, ki = x[:n2], x[n2:], k[:n2], k[n2:]
        y = jnp.concatenate([xr * kr - xi * ki, xr * ki + xi * kr], axis=0).astype(BF16)
        bm = jnp.dot(w3t_ref[k1], y, preferred_element_type=F32)
        _slab_store_block(b_ref, k1 * n2, n2, bm[:n2])
        im_block = plan.dump_block if k1 in (0, h1 - 1) else h1 - 1 + k1
        _slab_store_block(b_ref, im_block * n2, n2, bm[n2:])

    mi = mi_ref[...]
    skip = skip_ref[...]

    def last_stage(c, carry):
        tiles = [_slab_rows(b_ref, pl.multiple_of(q * n2 + c * g, g), g) for q in range(plan.nq)]
        rhs = jnp.concatenate(tiles, axis=0).astype(BF16)
        y = jnp.dot(mi, rhs, preferred_element_type=F32)
        for blk in range(nb):
            start = pl.multiple_of(blk * n2 + c * g, g)
            ucur = _slab_rows(u_ref, start, g)
            _slab_store_block(u_ref, start, g, y[blk * g:(blk + 1) * g] + skip * ucur)
        return carry

    lax.fori_loop(0, n2 // g, last_stage, 0, unroll=8)

    def gate_out(c, carry):
        r0 = pl.multiple_of(c * rows, rows)
        x0c = x0_ref[pl.ds(r0, rows), :].astype(F32)
        o_ref[pl.ds(r0, rows), :] = (x0c * _slab_rows(u_ref, r0, rows)).astype(o_ref.dtype)
        return carry

    lax.fori_loop(0, seq_len // rows, gate_out, 0)


def _hyena_operator(plan, proj3, skip, kspec, ct):
    bsz, seq_len, _ = proj3.shape
    c = skip.shape[-1]
    assert c % ct == 0 and ct % LANES == 0 and seq_len % COPY_ROWS == 0
    nct = c // ct
    nslab = ct // LANES
    n2, h1, pa = plan.n2, plan.h1, plan.pa
    part = lambda k: pl.BlockSpec((None, seq_len, ct), lambda j, b, k=k: (b, 0, k * nct + j))
    return pl.pallas_call(
        functools.partial(_hyena_kernel, plan),
        grid=(nct, bsz),
        in_specs=[
            part(0), part(1),
            pl.BlockSpec((1, ct), lambda j, b: (0, j)),
            pl.BlockSpec((h1, 2 * n2, ct), lambda j, b: (0, 0, j)),
            _const_spec(plan.w1.shape),
            _const_spec(plan.w3.shape),
            _const_spec(plan.w3t.shape),
            _const_spec(plan.mi.shape),
        ],
        out_specs=pl.BlockSpec((None, seq_len, ct), lambda j, b: (b, 0, j)),
        out_shape=jax.ShapeDtypeStruct((bsz, seq_len, c), BF16),
        scratch_shapes=[
            pltpu.VMEM((nslab, seq_len, LANES), F32),
            pltpu.VMEM((nslab, n2 * pa, LANES), F32),
            pltpu.VMEM((nslab, (plan.nq + 1) * n2, LANES), F32),
        ],
        compiler_params=_cparams(2),
        name="hyena_operator",
    )(proj3, proj3, skip.reshape(1, -1), kspec, plan.w1, plan.w3, plan.w3t, plan.mi)


def _merge_kernel(us_ref, vs_ref, ga_ref, gb_ref, ya_ref, x_ref, sg_ref, sw_ref, sbt_ref,
                  pa_ref, pb_ref, wo_ref, o_ref, yb_ref):
    heads, chunk, _ = sw_ref.shape
    tm, width = vs_ref.shape
    hd = width // heads
    ts = MERGE_SUB_ROWS
    for t0 in range(0, tm, ts):
        rows = slice(t0, t0 + ts)
        v = vs_ref[rows, :].astype(F32)
        ms = jnp.mean(v * v, axis=-1, keepdims=True)
        vn_all = (v * lax.rsqrt(ms + EPS) * sg_ref[...]).astype(BF16)
        for r in range(ts // chunk):
            rs = slice(t0 + r * chunk, t0 + (r + 1) * chunk)
            for g in range(heads):
                cs = slice(g * hd, (g + 1) * hd)
                s = (jnp.dot(sw_ref[g].astype(BF16), vn_all[r * chunk:(r + 1) * chunk, cs],
                             preferred_element_type=F32) + sbt_ref[:, g:g + 1])
                yb_ref[rs, cs] = (us_ref[rs, cs].astype(F32) * s).astype(BF16)
        pa = jnp.dot(ya_ref[rows, :], pa_ref[...].astype(BF16), preferred_element_type=F32)
        pb = jnp.dot(yb_ref[rows, :], pb_ref[...].astype(BF16), preferred_element_type=F32)
        merged = ga_ref[rows, :].astype(F32) * pa + gb_ref[rows, :].astype(F32) * pb
        o_ref[rows, :] = x_ref[rows, :] + jnp.dot(merged.astype(BF16), wo_ref[...].astype(BF16),
                                                  preferred_element_type=F32)


def _merge(proj, ya, x2d, sgu_g, sgu_w, sgu_b, pa_w, pb_w, wo_w, hy_cols, tm):
    t, d = x2d.shape
    width = sgu_g.shape[-1]
    heads, chunk, _ = sgu_w.shape
    assert t % tm == 0 and tm % MERGE_SUB_ROWS == 0 and MERGE_SUB_ROWS % chunk == 0
    assert hy_cols % width == 0 and ya.shape[1] == width
    base = hy_cols // width
    col = lambda k: pl.BlockSpec((tm, width), lambda i, k=k: (i, base + k))
    sbt = sgu_b.T
    return pl.pallas_call(
        _merge_kernel,
        grid=(t // tm,),
        in_specs=[
            col(0), col(1), col(2), col(3),
            pl.BlockSpec((tm, width), lambda i: (i, 0)),
            pl.BlockSpec((tm, d), lambda i: (i, 0)),
            _const_spec((1, width)),
            _const_spec(sgu_w.shape),
            _const_spec(sbt.shape),
            _const_spec(pa_w.shape),
            _const_spec(pb_w.shape),
            _const_spec(wo_w.shape),
        ],
        out_specs=pl.BlockSpec((tm, d), lambda i: (i, 0)),
        out_shape=jax.ShapeDtypeStruct((t, d), F32),
        scratch_shapes=[pltpu.VMEM((tm, width), BF16)],
        compiler_params=_cparams(1),
        name="sgu_merge",
    )(proj, proj, proj, proj, ya, x2d, sgu_g.reshape(1, -1), sgu_w, sbt, pa_w, pb_w, wo_w)


FFN_CHUNK = 256
FFN_SUB_ROWS = 512


def _ffn_kernel(xm_ref, xp_ref, xn_ref, g_ref, wup_ref, cw_ref, cb_ref, wdn_ref, fg_ref, o_ref,
                hs_ref, act_ref, *, tiles_per_seq, final_norm):
    tm, _ = xm_ref.shape
    hidden = wdn_ref.shape[0]
    halo = ROW_HALO
    i = pl.program_id(0)
    g = g_ref[...]

    def norm(xv):
        ms = jnp.mean(xv * xv, axis=-1, keepdims=True)
        return xv * lax.rsqrt(ms + EPS) * g

    at_start = (i % tiles_per_seq) == 0
    at_end = (i % tiles_per_seq) == tiles_per_seq - 1
    hs_ref[0:halo, :] = jnp.where(at_start, 0.0, norm(xp_ref[...])).astype(BF16)
    hs_ref[halo + tm:, :] = jnp.where(at_end, 0.0, norm(xn_ref[...])).astype(BF16)

    ts = FFN_SUB_ROWS
    ext = ts + 2 * halo
    for s in range(tm // ts):
        rs = slice(s * ts, (s + 1) * ts)
        hs_ref[halo + s * ts:halo + (s + 1) * ts, :] = norm(xm_ref[rs, :]).astype(BF16)
    for s in range(tm // ts):
        rs = slice(s * ts, (s + 1) * ts)
        h_ext = hs_ref[s * ts:s * ts + ext, :]
        h_main = hs_ref[halo + s * ts:halo + (s + 1) * ts, :]
        for k in range(hidden // FFN_CHUNK):
            cs = slice(k * FFN_CHUNK, (k + 1) * FFN_CHUNK)
            gs = slice(hidden + k * FFN_CHUNK, hidden + (k + 1) * FFN_CHUNK)
            a_ext = jnp.dot(h_ext, wup_ref[:, cs].astype(BF16), preferred_element_type=F32)
            gate = jnp.dot(h_main, wup_ref[:, gs].astype(BF16), preferred_element_type=F32)
            up = pltpu.roll(a_ext, 1, 0)[halo:halo + ts]
            dn = pltpu.roll(a_ext, ext - 1, 0)[halo:halo + ts]
            w = cw_ref[:, cs]
            c = w[0:1] * up + w[1:2] * a_ext[halo:halo + ts] + w[2:3] * dn + cb_ref[:, cs]
            act_ref[rs, cs] = (c / (1.0 + jnp.exp(-c)) * gate).astype(BF16)
        y = xm_ref[rs, :] + jnp.dot(act_ref[rs, :], wdn_ref[...].astype(BF16),
                                    preferred_element_type=F32)
        if final_norm:
            ms = jnp.mean(y * y, axis=-1, keepdims=True)
            y = y * lax.rsqrt(ms + EPS) * fg_ref[...]
        o_ref[rs, :] = y


def _ffn(x2d, seq_len, g, wup, conv_w, conv_b, wdn, final_g, final_norm, tm):
    t, d = x2d.shape
    hidden = wdn.shape[0]
    halo = ROW_HALO
    assert t % tm == 0 and seq_len % tm == 0 and tm % FFN_SUB_ROWS == 0 and FFN_SUB_ROWS % halo == 0
    assert hidden % FFN_CHUNK == 0
    per = tm // halo
    last = t // halo - 1
    kern = functools.partial(_ffn_kernel, tiles_per_seq=seq_len // tm, final_norm=final_norm)
    return pl.pallas_call(
        kern,
        grid=(t // tm,),
        in_specs=[
            pl.BlockSpec((tm, d), lambda i: (i, 0)),
            pl.BlockSpec((halo, d), lambda i: (jnp.maximum(i * per - 1, 0), 0)),
            pl.BlockSpec((halo, d), lambda i: (jnp.minimum((i + 1) * per, last), 0)),
            _const_spec((1, d)),
            _const_spec(wup.shape),
            _const_spec(conv_w.shape),
            _const_spec((1, hidden)),
            _const_spec(wdn.shape),
            _const_spec((1, d)),
        ],
        out_specs=pl.BlockSpec((tm, d), lambda i: (i, 0)),
        out_shape=jax.ShapeDtypeStruct((t, d), F32),
        scratch_shapes=[pltpu.VMEM((tm + 2 * halo, d), BF16), pltpu.VMEM((tm, hidden), BF16)],
        compiler_params=_cparams(1),
        name="ffn",
    )(x2d, x2d, x2d, g.reshape(1, -1), wup, conv_w, conv_b.reshape(1, -1), wdn,
      final_g.reshape(1, -1))


def kernel(x, norm1_g, w_in, hy_conv_w, hy_conv_b, filt_w1, filt_b1, filt_w2, filt_b2, filt_w3, filt_b3, filt_freq, filt_w4, hy_decay, hy_skip, sgu_norm_g, sgu_w, sgu_b, w_proj_hyena, w_proj_sgu, w_out, norm2_g, w_up, ffn_conv_w, ffn_conv_b, w_down, final_g):
    bsz, seq_len, d = x.shape
    depth = norm1_g.shape[0]
    hw = hy_skip.shape[-1]
    sw = sgu_norm_g.shape[-1]
    plan = _FftPlan(seq_len)
    x2d = x.reshape(bsz * seq_len, d)
    for i in range(depth):
        proj = _in_projection(x2d, seq_len, norm1_g[i].reshape(1, -1), w_in[i], hy_conv_w[i],
                              hy_conv_b[i], hw=hw, n_gelu_cols=2 * sw, tm=512, tn=512)
        filt = _implicit_filters(seq_len, filt_w1[i], filt_b1[i], filt_w2[i], filt_b2[i], filt_w3[i],
                                 filt_b3[i], filt_freq[i], filt_w4[i], hy_decay[i])
        kspec = _filter_spectrum(plan, filt, HYENA_CH)
        ya = _hyena_operator(plan, proj.reshape(bsz, seq_len, -1), hy_skip[i], kspec, HYENA_CH)
        x2d = _merge(proj, ya.reshape(bsz * seq_len, hw), x2d, sgu_norm_g[i], sgu_w[i],
                     sgu_b[i], w_proj_hyena[i], w_proj_sgu[i], w_out[i], hy_cols=2 * hw, tm=1024)
        x2d = _ffn(x2d, seq_len, norm2_g[i], w_up[i], ffn_conv_w[i], ffn_conv_b[i],
                   w_down[i], final_g, final_norm=(i == depth - 1), tm=1024)
    return x2d.reshape(bsz, seq_len, d)
```

```python
import functools
import math

import numpy as np
import jax
import jax.numpy as jnp
from jax import lax
from jax.experimental import pallas as pl
from jax.experimental.pallas import tpu as pltpu

EPS = 1e-6
F32 = jnp.float32
BF16 = jnp.bfloat16

LANES = 128
F32_SUBLANE_TILE = 8
BF16_SUBLANE_TILE = 16
VMEM_LIMIT_BYTES = 60 * 1024 * 1024

ROW_HALO = BF16_SUBLANE_TILE
FFT_INNER = 128
COPY_ROWS = 256
HYENA_CH = 256
MERGE_SUB_ROWS = 512


def _round_up(n, m):
    return -(-n // m) * m


def _cparams(n_axes):
    return pltpu.CompilerParams(
        dimension_semantics=("arbitrary",) * n_axes, vmem_limit_bytes=VMEM_LIMIT_BYTES)


def _const_spec(shape):
    nd = len(shape)
    return pl.BlockSpec(shape, lambda *_: (0,) * nd, pipeline_mode=pl.Buffered(1))


def _inproj_kernel(xm_ref, xp_ref, xn_ref, g_ref, w_ref, cw_ref, cb_ref, o_ref, hn_ref, stage_ref, *,
                   hw, n_gelu_cols, tn, tiles_per_seq):
    tm = xm_ref.shape[0]
    halo = ROW_HALO
    ext = tm + 2 * halo
    i = pl.program_id(0)
    g = g_ref[...]

    def norm(xv):
        ms = jnp.mean(xv * xv, axis=-1, keepdims=True)
        return xv * lax.rsqrt(ms + EPS) * g

    at_start = (i % tiles_per_seq) == 0
    at_end = (i % tiles_per_seq) == tiles_per_seq - 1
    hn_ref[0:halo, :] = jnp.where(at_start, 0.0, norm(xp_ref[...])).astype(BF16)
    hn_ref[halo:halo + tm, :] = norm(xm_ref[...]).astype(BF16)
    hn_ref[halo + tm:, :] = jnp.where(at_end, 0.0, norm(xn_ref[...])).astype(BF16)

    def conv_cols(col0, buf):
        cs = slice(col0, col0 + tn)
        acc = jnp.dot(hn_ref[...], w_ref[:, cs].astype(BF16), preferred_element_type=F32)
        stage = stage_ref.at[buf]
        _slab_store_block(stage, 0, ext, acc)
        up, mid, dn = (_slab_rows(stage, halo + shift, tm) for shift in (-1, 0, 1))
        w = cw_ref[:, cs]
        return w[0:1] * up + w[1:2] * mid + w[2:3] * dn + cb_ref[:, cs]

    for k in range(hw // tn):
        o_ref[:, k * tn:(k + 1) * tn] = conv_cols(k * tn, 0).astype(o_ref.dtype)
    for k in range(hw // tn):
        u = conv_cols(hw + k * tn, 0) * conv_cols(2 * hw + k * tn, 1)
        o_ref[:, hw + k * tn:hw + (k + 1) * tn] = u.astype(o_ref.dtype)
    for k in range((w_ref.shape[1] - 3 * hw) // tn):
        cs = slice(3 * hw + k * tn, 3 * hw + (k + 1) * tn)
        acc = jnp.dot(hn_ref[halo:halo + tm, :], w_ref[:, cs].astype(BF16), preferred_element_type=F32)
        if k * tn < n_gelu_cols:
            out = 0.5 * acc * (1.0 + lax.erf(acc * math.sqrt(0.5)))
        else:
            out = 1.0 / (1.0 + jnp.exp(-acc))
        o_ref[:, 2 * hw + k * tn:2 * hw + (k + 1) * tn] = out.astype(o_ref.dtype)


def _in_projection(x2d, seq_len, g, w_in, conv_w, conv_b, hw, n_gelu_cols, tm, tn):
    t, d = x2d.shape
    n = w_in.shape[1]
    halo = ROW_HALO
    assert t % tm == 0 and seq_len % tm == 0 and tm % halo == 0 and tn % LANES == 0
    assert hw % tn == 0 and n_gelu_cols % tn == 0 and (n - 3 * hw) % tn == 0
    per = tm // halo
    last = t // halo - 1
    kern = functools.partial(_inproj_kernel, hw=hw, n_gelu_cols=n_gelu_cols, tn=tn,
                             tiles_per_seq=seq_len // tm)
    return pl.pallas_call(
        kern,
        grid=(t // tm,),
        in_specs=[
            pl.BlockSpec((tm, d), lambda i: (i, 0)),
            pl.BlockSpec((halo, d), lambda i: (jnp.maximum(i * per - 1, 0), 0)),
            pl.BlockSpec((halo, d), lambda i: (jnp.minimum((i + 1) * per, last), 0)),
            _const_spec((1, d)),
            _const_spec(w_in.shape),
            _const_spec(conv_w.shape),
            _const_spec((1, conv_b.shape[-1])),
        ],
        out_specs=pl.BlockSpec((tm, n - hw), lambda i: (i, 0)),
        out_shape=jax.ShapeDtypeStruct((t, n - hw), BF16),
        scratch_shapes=[pltpu.VMEM((tm + 2 * halo, d), BF16),
                        pltpu.VMEM((2, tn // LANES, tm + 2 * halo, LANES), F32)],
        compiler_params=_cparams(1),
        name="in_projection",
    )(x2d, x2d, x2d, g, w_in, conv_w, conv_b.reshape(1, -1))


def _filter_kernel(w1t_ref, w1c_ref, w1s_ref, b1_ref, w2_ref, b2_ref, w3_ref, b3_ref, fr_ref,
                   w4_ref, dec_ref, o_ref, *, seq_len, bands):
    tl = o_ref.shape[0]
    hi = lax.Precision.HIGHEST
    dot = functools.partial(jnp.dot, preferred_element_type=F32, precision=hi)
    r0 = pl.program_id(0) * tl
    pos = (r0 + lax.broadcasted_iota(jnp.int32, (1, tl), 1)).astype(F32)
    t = pos / float(seq_len - 1)
    band_step = (bands - 1 - 1e-4) / (bands - 1)
    band = 1e-4 + band_step * lax.broadcasted_iota(jnp.int32, (bands, 1), 0).astype(F32)
    phase = ((2.0 * math.pi / seq_len) * pos) * band
    a = fr_ref[...]
    z1 = w1t_ref[...] * t + dot(w1c_ref[...], jnp.cos(phase)) - dot(w1s_ref[...], jnp.sin(phase))
    h = jnp.sin(a * (z1 + b1_ref[...]))
    h = jnp.sin(a * (dot(w2_ref[...], h) + b2_ref[...]))
    h = jnp.sin(a * (dot(w3_ref[...], h) + b3_ref[...]))
    def split(v):
        v_hi = v.astype(BF16)
        return v_hi, (v - v_hi.astype(F32)).astype(BF16)

    h_hi, h_lo = split(h)
    w_hi, w_lo = split(w4_ref[...])
    lhs = jnp.concatenate([h_hi, h_lo, h_hi, h_lo], axis=0)
    rhs = jnp.concatenate([w_hi, w_hi, w_lo, w_lo], axis=0)
    f = lax.dot_general(lhs, rhs, (((0,), (0,)), ((), ())), preferred_element_type=F32)
    t_col = (r0 + lax.broadcasted_iota(jnp.int32, (tl, 1), 0)).astype(F32) / float(seq_len - 1)
    o_ref[...] = f * jnp.exp(-t_col * jnp.abs(dec_ref[...]))


def _implicit_filters(seq_len, w1, b1, w2, b2, w3, b3, freq, w4, decay, tl=512):
    bands = (w1.shape[0] - 1) // 2
    hid = w1.shape[1]
    n_out = w4.shape[1]
    assert seq_len % tl == 0
    col = lambda v: v.reshape(-1, 1)
    args = (w1[0:1].T, w1[1:1 + bands].T, w1[1 + bands:].T, col(b1), w2.T, col(b2), w3.T, col(b3),
            col(freq), w4, decay.reshape(1, -1))
    full = lambda a: pl.BlockSpec(a.shape, lambda i: (0, 0))
    kern = functools.partial(_filter_kernel, seq_len=seq_len, bands=bands)
    del hid
    return pl.pallas_call(
        kern,
        grid=(seq_len // tl,),
        in_specs=[full(a) for a in args],
        out_specs=pl.BlockSpec((tl, n_out), lambda i: (i, 0)),
        out_shape=jax.ShapeDtypeStruct((seq_len, n_out), F32),
        compiler_params=_cparams(1),
        name="implicit_filter",
    )(*args)


class _FftPlan:
    def __init__(self, seq_len):
        n2 = FFT_INNER
        assert seq_len % n2 == 0
        self.seq_len = seq_len
        self.n = 2 * seq_len
        self.n2 = n2
        self.n1 = self.n // n2
        self.nb = seq_len // n2
        self.h1 = self.n1 // 2 + 1
        self.pa = 2 * self.h1
        self.nq = self.n1
        self.dump_block = self.n1
        assert self.nb % BF16_SUBLANE_TILE == 0
        n1, nb, h1, n = self.n1, self.nb, self.h1, self.n
        k1 = np.arange(h1)[:, None]
        ph = 2.0 * np.pi * k1 * np.arange(nb)[None, :] / n1
        w1 = np.zeros((self.pa, nb))
        w1[:h1] = np.cos(ph)
        w1[h1:2 * h1] = -np.sin(ph)
        k2 = np.arange(n2)[:, None]
        m2 = np.arange(n2)[None, :]
        w3 = np.zeros((h1, 2 * n2, 2 * n2))
        for k in range(h1):
            th = 2.0 * np.pi * (((k + n1 * k2) * m2) % n) / n
            er, ei = np.cos(th), -np.sin(th)
            w3[k] = np.block([[er, -ei], [ei, er]])
        ck = np.full(h1, 2.0)
        ck[0] = 1.0
        ck[-1] = 1.0
        phi = 2.0 * np.pi * np.arange(nb)[:, None] * np.arange(h1)[None, :] / n1
        mi = np.zeros((nb, self.nq))
        mi[:, :h1] = ck * np.cos(phi) / n
        mi[:, h1:] = (-ck * np.sin(phi) / n)[:, 1:h1 - 1]
        g = F32_SUBLANE_TILE
        w1k = np.einsum("kn,ij->kinj", w1, np.eye(g)).reshape(self.pa * g, nb * g)
        self.w1 = jnp.asarray(w1k, F32).astype(BF16)
        self.w3 = jnp.asarray(w3, F32).astype(BF16)
        self.w3t = jnp.asarray(np.transpose(w3, (0, 2, 1)), F32).astype(BF16)
        mik = np.einsum("nq,ij->niqj", mi, np.eye(g)).reshape(nb * g, self.nq * g)
        self.mi = jnp.asarray(mik, F32).astype(BF16)


def _slab_store_block(ref, start, size, val):
    for s in range(ref.shape[0]):
        ref[s, pl.ds(start, size), :] = val[:, s * LANES:(s + 1) * LANES]


def _slab_rows(ref, start, size):
    return jnp.concatenate([ref[s, pl.ds(start, size), :] for s in range(ref.shape[0])], axis=1)


def _fft_stage1(plan, load_rows, a_ref, w1_ref):
    w1 = w1_ref[...]
    g = F32_SUBLANE_TILE

    def body(c, carry):
        tiles = [load_rows(pl.multiple_of(blk * plan.n2 + c * g, g), g) for blk in range(plan.nb)]
        rhs = jnp.concatenate(tiles, axis=0).astype(BF16)
        a = jnp.dot(w1, rhs, preferred_element_type=F32)
        for k in range(plan.pa):
            _slab_store_block(a_ref, pl.multiple_of(k * plan.n2 + c * g, g), g, a[k * g:(k + 1) * g])
        return carry

    lax.fori_loop(0, plan.n2 // g, body, 0, unroll=8)


def _fft_stage2(plan, a_ref, w3_ref, k1):
    n2 = plan.n2
    aligned = (lambda r: r) if isinstance(k1, int) else (lambda r: pl.multiple_of(r, n2))
    re = _slab_rows(a_ref, aligned(k1 * n2), n2)
    im = _slab_rows(a_ref, aligned((plan.h1 + k1) * n2), n2)
    rhs = jnp.concatenate([re, im], axis=0).astype(BF16)
    return jnp.dot(w3_ref[k1], rhs, preferred_element_type=F32)


def _spectrum_kernel(plan, ff_ref, fb_ref, w1_ref, w3_ref, o_ref, af_ref, ab_ref):
    n2 = plan.n2
    _fft_stage1(plan, lambda start, size: ff_ref[pl.ds(start, size), :], af_ref, w1_ref)
    _fft_stage1(plan, lambda start, size: fb_ref[pl.ds(start, size), :], ab_ref, w1_ref)
    hb0 = fb_ref[0:1, :]

    def body(k1, carry):
        xf = _fft_stage2(plan, af_ref, w3_ref, k1)
        xb = _fft_stage2(plan, ab_ref, w3_ref, k1)
        o_ref[k1, 0:n2, :] = (xf[:n2] + xb[:n2] - hb0).astype(o_ref.dtype)
        o_ref[k1, n2:2 * n2, :] = (xf[n2:] - xb[n2:]).astype(o_ref.dtype)
        return carry

    lax.fori_loop(0, plan.h1, body, 0, unroll=11)


def _filter_spectrum(plan, filt, ct):
    seq_len, c2 = filt.shape
    c = c2 // 2
    assert c % ct == 0 and ct % LANES == 0
    nct = c // ct
    nslab = ct // LANES
    n2, h1, pa = plan.n2, plan.h1, plan.pa
    return pl.pallas_call(
        functools.partial(_spectrum_kernel, plan),
        grid=(nct,),
        in_specs=[
            pl.BlockSpec((seq_len, ct), lambda j: (0, j)),
            pl.BlockSpec((seq_len, ct), lambda j: (0, nct + j)),
            _const_spec(plan.w1.shape),
            _const_spec(plan.w3.shape),
        ],
        out_specs=pl.BlockSpec((h1, 2 * n2, ct), lambda j: (0, 0, j)),
        out_shape=jax.ShapeDtypeStruct((h1, 2 * n2, c), BF16),
        scratch_shapes=[
            pltpu.VMEM((nslab, n2 * pa, LANES), F32),
            pltpu.VMEM((nslab, n2 * pa, LANES), F32),
        ],
        compiler_params=_cparams(1),
        name="filter_spectrum",
    )(filt, filt, plan.w1, plan.w3)


def _hyena_kernel(plan, x0_ref, uin_ref, skip_ref, ks_ref, w1_ref, w3_ref, w3t_ref, mi_ref, o_ref,
                  u_ref, a_ref, b_ref):
    seq_len, n2, nb, h1 = plan.seq_len, plan.n2, plan.nb, plan.h1
    rows = COPY_ROWS
    g = F32_SUBLANE_TILE

    def load_u(c, carry):
        r0 = pl.multiple_of(c * rows, rows)
        _slab_store_block(u_ref, r0, rows, uin_ref[pl.ds(r0, rows), :].astype(F32))
        return carry

    lax.fori_loop(0, seq_len // rows, load_u, 0)

    _fft_stage1(plan, functools.partial(_slab_rows, u_ref), a_ref, w1_ref)

    for k1 in range(h1):
        x = _fft_stage2(plan, a_ref, w3_ref, k1)
        k = ks_ref[k1].astype(F32)
        xr, xi, kr, ki = x[:n2], x[n2:], k[:n2], k[n2:]
        y = jnp.concatenate([xr * kr - xi * ki, xr * ki + xi * kr], axis=0).astype(BF16)
        bm = jnp.dot(w3t_ref[k1], y, preferred_element_type=F32)
        _slab_store_block(b_ref, k1 * n2, n2, bm[:n2])
        im_block = plan.dump_block if k1 in (0, h1 - 1) else h1 - 1 + k1
        _slab_store_block(b_ref, im_block * n2, n2, bm[n2:])

    mi = mi_ref[...]
    skip = skip_ref[...]

    def last_stage(c, carry):
        tiles = [_slab_rows(b_ref, pl.multiple_of(q * n2 + c * g, g), g) for q in range(plan.nq)]
        rhs = jnp.concatenate(tiles, axis=0).astype(BF16)
        y = jnp.dot(mi, rhs, preferred_element_type=F32)
        for blk in range(nb):
            start = pl.multiple_of(blk * n2 + c * g, g)
            ucur = _slab_rows(u_ref, start, g)
            _slab_store_block(u_ref, start, g, y[blk * g:(blk + 1) * g] + skip * ucur)
        return carry

    lax.fori_loop(0, n2 // g, last_stage, 0, unroll=8)

    def gate_out(c, carry):
        r0 = pl.multiple_of(c * rows, rows)
        x0c = x0_ref[pl.ds(r0, rows), :].astype(F32)
        o_ref[pl.ds(r0, rows), :] = (x0c * _slab_rows(u_ref, r0, rows)).astype(o_ref.dtype)
        return carry

    lax.fori_loop(0, seq_len // rows, gate_out, 0)


def _hyena_operator(plan, proj3, skip, kspec, ct):
    bsz, seq_len, _ = proj3.shape
    c = skip.shape[-1]
    assert c % ct == 0 and ct % LANES == 0 and seq_len % COPY_ROWS == 0
    nct = c // ct
    nslab = ct // LANES
    n2, h1, pa = plan.n2, plan.h1, plan.pa
    part = lambda k: pl.BlockSpec((None, seq_len, ct), lambda j, b, k=k: (b, 0, k * nct + j))
    return pl.pallas_call(
        functools.partial(_hyena_kernel, plan),
        grid=(nct, bsz),
        in_specs=[
            part(0), part(1),
            pl.BlockSpec((1, ct), lambda j, b: (0, j)),
            pl.BlockSpec((h1, 2 * n2, ct), lambda j, b: (0, 0, j)),
            _const_spec(plan.w1.shape),
            _const_spec(plan.w3.shape),
            _const_spec(plan.w3t.shape),
            _const_spec(plan.mi.shape),
        ],
        out_specs=pl.BlockSpec((None, seq_len, ct), lambda j, b: (b, 0, j)),
        out_shape=jax.ShapeDtypeStruct((bsz, seq_len, c), BF16),
        scratch_shapes=[
            pltpu.VMEM((nslab, seq_len, LANES), F32),
            pltpu.VMEM((nslab, n2 * pa, LANES), F32),
            pltpu.VMEM((nslab, (plan.nq + 1) * n2, LANES), F32),
        ],
        compiler_params=_cparams(2),
        name="hyena_operator",
    )(proj3, proj3, skip.reshape(1, -1), kspec, plan.w1, plan.w3, plan.w3t, plan.mi)


def _merge_kernel(us_ref, vs_ref, ga_ref, gb_ref, ya_ref, x_ref, sg_ref, sw_ref, sbt_ref,
                  pa_ref, pb_ref, wo_ref, o_ref, yb_ref):
    heads, chunk, _ = sw_ref.shape
    tm, width = vs_ref.shape
    hd = width // heads
    ts = MERGE_SUB_ROWS
    for t0 in range(0, tm, ts):
        rows = slice(t0, t0 + ts)
        v = vs_ref[rows, :].astype(F32)
        ms = jnp.mean(v * v, axis=-1, keepdims=True)
        vn_all = (v * lax.rsqrt(ms + EPS) * sg_ref[...]).astype(BF16)
        for r in range(ts // chunk):
            rs = slice(t0 + r * chunk, t0 + (r + 1) * chunk)
            for g in range(heads):
                cs = slice(g * hd, (g + 1) * hd)
                s = (jnp.dot(sw_ref[g].astype(BF16), vn_all[r * chunk:(r + 1) * chunk, cs],
                             preferred_element_type=F32) + sbt_ref[:, g:g + 1])
                yb_ref[rs, cs] = (us_ref[rs, cs].astype(F32) * s).astype(BF16)
        pa = jnp.dot(ya_ref[rows, :], pa_ref[...].astype(BF16), preferred_element_type=F32)
        pb = jnp.dot(yb_ref[rows, :], pb_ref[...].astype(BF16), preferred_element_type=F32)
        merged = ga_ref[rows, :].astype(F32) * pa + gb_ref[rows, :].astype(F32) * pb
        o_ref[rows, :] = x_ref[rows, :] + jnp.dot(merged.astype(BF16), wo_ref[...].astype(BF16),
                                                  preferred_element_type=F32)


def _merge(proj, ya, x2d, sgu_g, sgu_w, sgu_b, pa_w, pb_w, wo_w, hy_cols, tm):
    t, d = x2d.shape
    width = sgu_g.shape[-1]
    heads, chunk, _ = sgu_w.shape
    assert t % tm == 0 and tm % MERGE_SUB_ROWS == 0 and MERGE_SUB_ROWS % chunk == 0
    assert hy_cols % width == 0 and ya.shape[1] == width
    base = hy_cols // width
    col = lambda k: pl.BlockSpec((tm, width), lambda i, k=k: (i, base + k))
    sbt = sgu_b.T
    return pl.pallas_call(
        _merge_kernel,
        grid=(t // tm,),
        in_specs=[
            col(0), col(1), col(2), col(3),
            pl.BlockSpec((tm, width), lambda i: (i, 0)),
            pl.BlockSpec((tm, d), lambda i: (i, 0)),
            _const_spec((1, width)),
            _const_spec(sgu_w.shape),
            _const_spec(sbt.shape),
            _const_spec(pa_w.shape),
            _const_spec(pb_w.shape),
            _const_spec(wo_w.shape),
        ],
        out_specs=pl.BlockSpec((tm, d), lambda i: (i, 0)),
        out_shape=jax.ShapeDtypeStruct((t, d), F32),
        scratch_shapes=[pltpu.VMEM((tm, width), BF16)],
        compiler_params=_cparams(1),
        name="sgu_merge",
    )(proj, proj, proj, proj, ya, x2d, sgu_g.reshape(1, -1), sgu_w, sbt, pa_w, pb_w, wo_w)


FFN_CHUNK = 256
FFN_SUB_ROWS = 512


def _ffn_kernel(xm_ref, xp_ref, xn_ref, g_ref, wup_ref, cw_ref, cb_ref, wdn_ref, fg_ref, o_ref,
                hs_ref, act_ref, stage_ref, *, tiles_per_seq, final_norm):
    tm, _ = xm_ref.shape
    hidden = wdn_ref.shape[0]
    halo = ROW_HALO
    i = pl.program_id(0)
    g = g_ref[...]

    def norm(xv):
        ms = jnp.mean(xv * xv, axis=-1, keepdims=True)
        return xv * lax.rsqrt(ms + EPS) * g

    at_start = (i % tiles_per_seq) == 0
    at_end = (i % tiles_per_seq) == tiles_per_seq - 1
    hs_ref[0:halo, :] = jnp.where(at_start, 0.0, norm(xp_ref[...])).astype(BF16)
    hs_ref[halo + tm:, :] = jnp.where(at_end, 0.0, norm(xn_ref[...])).astype(BF16)

    ts = FFN_SUB_ROWS
    ext = ts + 2 * halo
    for s in range(tm // ts):
        rs = slice(s * ts, (s + 1) * ts)
        hs_ref[halo + s * ts:halo + (s + 1) * ts, :] = norm(xm_ref[rs, :]).astype(BF16)
    for s in range(tm // ts):
        rs = slice(s * ts, (s + 1) * ts)
        h_ext = hs_ref[s * ts:s * ts + ext, :]
        h_main = hs_ref[halo + s * ts:halo + (s + 1) * ts, :]
        for k in range(hidden // FFN_CHUNK):
            cs = slice(k * FFN_CHUNK, (k + 1) * FFN_CHUNK)
            gs = slice(hidden + k * FFN_CHUNK, hidden + (k + 1) * FFN_CHUNK)
            a_ext = jnp.dot(h_ext, wup_ref[:, cs].astype(BF16), preferred_element_type=F32)
            gate = jnp.dot(h_main, wup_ref[:, gs].astype(BF16), preferred_element_type=F32)
            stage = stage_ref.at[k % 2]
            _slab_store_block(stage, 0, ext, a_ext)
            up, mid, dn = (_slab_rows(stage, halo + shift, ts) for shift in (-1, 0, 1))
            w = cw_ref[:, cs]
            c = w[0:1] * up + w[1:2] * mid + w[2:3] * dn + cb_ref[:, cs]
            act_ref[rs, cs] = (c / (1.0 + jnp.exp(-c)) * gate).astype(BF16)
        y = xm_ref[rs, :] + jnp.dot(act_ref[rs, :], wdn_ref[...].astype(BF16),
                                    preferred_element_type=F32)
        if final_norm:
            ms = jnp.mean(y * y, axis=-1, keepdims=True)
            y = y * lax.rsqrt(ms + EPS) * fg_ref[...]
        o_ref[rs, :] = y


def _ffn(x2d, seq_len, g, wup, conv_w, conv_b, wdn, final_g, final_norm, tm):
    t, d = x2d.shape
    hidden = wdn.shape[0]
    halo = ROW_HALO
    assert t % tm == 0 and seq_len % tm == 0 and tm % FFN_SUB_ROWS == 0 and FFN_SUB_ROWS % halo == 0
    assert hidden % FFN_CHUNK == 0
    per = tm // halo
    last = t // halo - 1
    kern = functools.partial(_ffn_kernel, tiles_per_seq=seq_len // tm, final_norm=final_norm)
    return pl.pallas_call(
        kern,
        grid=(t // tm,),
        in_specs=[
            pl.BlockSpec((tm, d), lambda i: (i, 0)),
            pl.BlockSpec((halo, d), lambda i: (jnp.maximum(i * per - 1, 0), 0)),
            pl.BlockSpec((halo, d), lambda i: (jnp.minimum((i + 1) * per, last), 0)),
            _const_spec((1, d)),
            _const_spec(wup.shape),
            _const_spec(conv_w.shape),
            _const_spec((1, hidden)),
            _const_spec(wdn.shape),
            _const_spec((1, d)),
        ],
        out_specs=pl.BlockSpec((tm, d), lambda i: (i, 0)),
        out_shape=jax.ShapeDtypeStruct((t, d), F32),
        scratch_shapes=[pltpu.VMEM((tm + 2 * halo, d), BF16), pltpu.VMEM((tm, hidden), BF16),
                        pltpu.VMEM((2, FFN_CHUNK // LANES, FFN_SUB_ROWS + 2 * halo, LANES), F32)],
        compiler_params=_cparams(1),
        name="ffn",
    )(x2d, x2d, x2d, g.reshape(1, -1), wup, conv_w, conv_b.reshape(1, -1), wdn,
      final_g.reshape(1, -1))


def kernel(x, norm1_g, w_in, hy_conv_w, hy_conv_b, filt_w1, filt_b1, filt_w2, filt_b2, filt_w3, filt_b3, filt_freq, filt_w4, hy_decay, hy_skip, sgu_norm_g, sgu_w, sgu_b, w_proj_hyena, w_proj_sgu, w_out, norm2_g, w_up, ffn_conv_w, ffn_conv_b, w_down, final_g):
    bsz, seq_len, d = x.shape
    depth = norm1_g.shape[0]
    hw = hy_skip.shape[-1]
    sw = sgu_norm_g.shape[-1]
    plan = _FftPlan(seq_len)
    x2d = x.reshape(bsz * seq_len, d)
    for i in range(depth):
        proj = _in_projection(x2d, seq_len, norm1_g[i].reshape(1, -1), w_in[i], hy_conv_w[i],
                              hy_conv_b[i], hw=hw, n_gelu_cols=2 * sw, tm=512, tn=512)
        filt = _implicit_filters(seq_len, filt_w1[i], filt_b1[i], filt_w2[i], filt_b2[i], filt_w3[i],
                                 filt_b3[i], filt_freq[i], filt_w4[i], hy_decay[i])
        kspec = _filter_spectrum(plan, filt, HYENA_CH)
        ya = _hyena_operator(plan, proj.reshape(bsz, seq_len, -1), hy_skip[i], kspec, HYENA_CH)
        x2d = _merge(proj, ya.reshape(bsz * seq_len, hw), x2d, sgu_norm_g[i], sgu_w[i],
                     sgu_b[i], w_proj_hyena[i], w_proj_sgu[i], w_out[i], hy_cols=2 * hw, tm=1024)
        x2d = _ffn(x2d, seq_len, norm2_g[i], w_up[i], ffn_conv_w[i], ffn_conv_b[i],
                   w_down[i], final_g, final_norm=(i == depth - 1), tm=1024)
    return x2d.reshape(bsz, seq_len, d)
```

```python
import functools
import math

import numpy as np
import jax
import jax.numpy as jnp
from jax import lax
from jax.experimental import pallas as pl
from jax.experimental.pallas import tpu as pltpu

EPS = 1e-6
F32 = jnp.float32
BF16 = jnp.bfloat16

LANES = 128
F32_SUBLANE_TILE = 8
BF16_SUBLANE_TILE = 16
VMEM_LIMIT_BYTES = 60 * 1024 * 1024

ROW_HALO = BF16_SUBLANE_TILE
FFT_INNER = 128
COPY_ROWS = 256
HYENA_CH = 256
MERGE_SUB_ROWS = 512


def _round_up(n, m):
    return -(-n // m) * m


def _cparams(n_axes):
    return pltpu.CompilerParams(
        dimension_semantics=("arbitrary",) * n_axes, vmem_limit_bytes=VMEM_LIMIT_BYTES)


def _const_spec(shape):
    nd = len(shape)
    return pl.BlockSpec(shape, lambda *_: (0,) * nd, pipeline_mode=pl.Buffered(1))


def _inproj_kernel(xm_ref, xp_ref, xn_ref, g_ref, w_ref, cw_ref, cb_ref, o_ref, hn_ref, stage_ref, *,
                   hw, n_gelu_cols, tn, tiles_per_seq):
    tm = xm_ref.shape[0]
    halo = ROW_HALO
    ext = tm + 2 * halo
    i = pl.program_id(0)
    g = g_ref[...]

    def norm(xv):
        ms = jnp.mean(xv * xv, axis=-1, keepdims=True)
        return xv * lax.rsqrt(ms + EPS) * g

    at_start = (i % tiles_per_seq) == 0
    at_end = (i % tiles_per_seq) == tiles_per_seq - 1
    hn_ref[0:halo, :] = jnp.where(at_start, 0.0, norm(xp_ref[...])).astype(BF16)
    hn_ref[halo:halo + tm, :] = norm(xm_ref[...]).astype(BF16)
    hn_ref[halo + tm:, :] = jnp.where(at_end, 0.0, norm(xn_ref[...])).astype(BF16)

    def conv_cols(col0, buf):
        cs = slice(col0, col0 + tn)
        acc = jnp.dot(hn_ref[...], w_ref[:, cs].astype(BF16), preferred_element_type=F32)
        stage = stage_ref.at[buf]
        _slab_store_block(stage, 0, ext, acc)
        up, mid, dn = (_slab_rows(stage, halo + shift, tm) for shift in (-1, 0, 1))
        w = cw_ref[:, cs]
        return w[0:1] * up + w[1:2] * mid + w[2:3] * dn + cb_ref[:, cs]

    for k in range(hw // tn):
        o_ref[:, k * tn:(k + 1) * tn] = conv_cols(k * tn, 0).astype(o_ref.dtype)
    for k in range(hw // tn):
        u = conv_cols(hw + k * tn, 0) * conv_cols(2 * hw + k * tn, 1)
        o_ref[:, hw + k * tn:hw + (k + 1) * tn] = u.astype(o_ref.dtype)
    for k in range((w_ref.shape[1] - 3 * hw) // tn):
        cs = slice(3 * hw + k * tn, 3 * hw + (k + 1) * tn)
        acc = jnp.dot(hn_ref[halo:halo + tm, :], w_ref[:, cs].astype(BF16), preferred_element_type=F32)
        if k * tn < n_gelu_cols:
            out = 0.5 * acc * (1.0 + lax.erf(acc * math.sqrt(0.5)))
        else:
            out = 1.0 / (1.0 + jnp.exp(-acc))
        o_ref[:, 2 * hw + k * tn:2 * hw + (k + 1) * tn] = out.astype(o_ref.dtype)


def _in_projection(x2d, seq_len, g, w_in, conv_w, conv_b, hw, n_gelu_cols, tm, tn):
    t, d = x2d.shape
    n = w_in.shape[1]
    halo = ROW_HALO
    assert t % tm == 0 and seq_len % tm == 0 and tm % halo == 0 and tn % LANES == 0
    assert hw % tn == 0 and n_gelu_cols % tn == 0 and (n - 3 * hw) % tn == 0
    per = tm // halo
    last = t // halo - 1
    kern = functools.partial(_inproj_kernel, hw=hw, n_gelu_cols=n_gelu_cols, tn=tn,
                             tiles_per_seq=seq_len // tm)
    return pl.pallas_call(
        kern,
        grid=(t // tm,),
        in_specs=[
            pl.BlockSpec((tm, d), lambda i: (i, 0)),
            pl.BlockSpec((halo, d), lambda i: (jnp.maximum(i * per - 1, 0), 0)),
            pl.BlockSpec((halo, d), lambda i: (jnp.minimum((i + 1) * per, last), 0)),
            _const_spec((1, d)),
            _const_spec(w_in.shape),
            _const_spec(conv_w.shape),
            _const_spec((1, conv_b.shape[-1])),
        ],
        out_specs=pl.BlockSpec((tm, n - hw), lambda i: (i, 0)),
        out_shape=jax.ShapeDtypeStruct((t, n - hw), BF16),
        scratch_shapes=[pltpu.VMEM((tm + 2 * halo, d), BF16),
                        pltpu.VMEM((2, tn // LANES, tm + 2 * halo, LANES), F32)],
        compiler_params=_cparams(1),
        name="in_projection",
    )(x2d, x2d, x2d, g, w_in, conv_w, conv_b.reshape(1, -1))


def _filter_kernel(w1t_ref, w1c_ref, w1s_ref, b1_ref, w2_ref, b2_ref, w3_ref, b3_ref, fr_ref,
                   w4_ref, dec_ref, o_ref, *, seq_len, bands):
    tl = o_ref.shape[0]
    hi = lax.Precision.HIGHEST
    dot = functools.partial(jnp.dot, preferred_element_type=F32, precision=hi)
    r0 = pl.program_id(0) * tl
    pos = (r0 + lax.broadcasted_iota(jnp.int32, (1, tl), 1)).astype(F32)
    t = pos / float(seq_len - 1)
    band_step = (bands - 1 - 1e-4) / (bands - 1)
    band = 1e-4 + band_step * lax.broadcasted_iota(jnp.int32, (bands, 1), 0).astype(F32)
    phase = ((2.0 * math.pi / seq_len) * pos) * band
    a = fr_ref[...]
    z1 = w1t_ref[...] * t + dot(w1c_ref[...], jnp.cos(phase)) - dot(w1s_ref[...], jnp.sin(phase))
    h = jnp.sin(a * (z1 + b1_ref[...]))
    h = jnp.sin(a * (dot(w2_ref[...], h) + b2_ref[...]))
    h = jnp.sin(a * (dot(w3_ref[...], h) + b3_ref[...]))
    def split(v):
        v_hi = v.astype(BF16)
        return v_hi, (v - v_hi.astype(F32)).astype(BF16)

    h_hi, h_lo = split(h)
    w_hi, w_lo = split(w4_ref[...])
    lhs = jnp.concatenate([h_hi, h_lo, h_hi, h_lo], axis=0)
    rhs = jnp.concatenate([w_hi, w_hi, w_lo, w_lo], axis=0)
    f = lax.dot_general(lhs, rhs, (((0,), (0,)), ((), ())), preferred_element_type=F32)
    t_col = (r0 + lax.broadcasted_iota(jnp.int32, (tl, 1), 0)).astype(F32) / float(seq_len - 1)
    o_ref[...] = f * jnp.exp(-t_col * jnp.abs(dec_ref[...]))


def _implicit_filters(seq_len, w1, b1, w2, b2, w3, b3, freq, w4, decay, tl=512):
    bands = (w1.shape[0] - 1) // 2
    hid = w1.shape[1]
    n_out = w4.shape[1]
    assert seq_len % tl == 0
    col = lambda v: v.reshape(-1, 1)
    args = (w1[0:1].T, w1[1:1 + bands].T, w1[1 + bands:].T, col(b1), w2.T, col(b2), w3.T, col(b3),
            col(freq), w4, decay.reshape(1, -1))
    full = lambda a: pl.BlockSpec(a.shape, lambda i: (0, 0))
    kern = functools.partial(_filter_kernel, seq_len=seq_len, bands=bands)
    del hid
    return pl.pallas_call(
        kern,
        grid=(seq_len // tl,),
        in_specs=[full(a) for a in args],
        out_specs=pl.BlockSpec((tl, n_out), lambda i: (i, 0)),
        out_shape=jax.ShapeDtypeStruct((seq_len, n_out), F32),
        compiler_params=_cparams(1),
        name="implicit_filter",
    )(*args)


class _FftPlan:
    def __init__(self, seq_len):
        n2 = FFT_INNER
        assert seq_len % n2 == 0
        self.seq_len = seq_len
        self.n = 2 * seq_len
        self.n2 = n2
        self.n1 = self.n // n2
        self.nb = seq_len // n2
        self.h1 = self.n1 // 2 + 1
        self.pa = 2 * self.h1
        self.nq = self.n1
        self.dump_block = self.n1
        assert self.nb % BF16_SUBLANE_TILE == 0
        n1, nb, h1, n = self.n1, self.nb, self.h1, self.n
        k1 = np.arange(h1)[:, None]
        ph = 2.0 * np.pi * k1 * np.arange(nb)[None, :] / n1
        w1 = np.zeros((self.pa, nb))
        w1[:h1] = np.cos(ph)
        w1[h1:2 * h1] = -np.sin(ph)
        k2 = np.arange(n2)[:, None]
        m2 = np.arange(n2)[None, :]
        w3 = np.zeros((h1, 2 * n2, 2 * n2))
        for k in range(h1):
            th = 2.0 * np.pi * (((k + n1 * k2) * m2) % n) / n
            er, ei = np.cos(th), -np.sin(th)
            w3[k] = np.block([[er, -ei], [ei, er]])
        ck = np.full(h1, 2.0)
        ck[0] = 1.0
        ck[-1] = 1.0
        phi = 2.0 * np.pi * np.arange(nb)[:, None] * np.arange(h1)[None, :] / n1
        mi = np.zeros((nb, self.nq))
        mi[:, :h1] = ck * np.cos(phi) / n
        mi[:, h1:] = (-ck * np.sin(phi) / n)[:, 1:h1 - 1]
        g = F32_SUBLANE_TILE
        w1k = np.einsum("kn,ij->kinj", w1, np.eye(g)).reshape(self.pa * g, nb * g)
        self.w1 = jnp.asarray(w1k, F32).astype(BF16)
        self.w3 = jnp.asarray(w3, F32).astype(BF16)
        self.w3t = jnp.asarray(np.transpose(w3, (0, 2, 1)), F32).astype(BF16)
        mik = np.einsum("nq,ij->niqj", mi, np.eye(g)).reshape(nb * g, self.nq * g)
        self.mi = jnp.asarray(mik, F32).astype(BF16)


def _slab_store_block(ref, start, size, val):
    for s in range(ref.shape[0]):
        ref[s, pl.ds(start, size), :] = val[:, s * LANES:(s + 1) * LANES]


def _slab_rows(ref, start, size):
    return jnp.concatenate([ref[s, pl.ds(start, size), :] for s in range(ref.shape[0])], axis=1)


def _fft_stage1(plan, load_rows, a_ref, w1_ref):
    w1 = w1_ref[...]
    g = F32_SUBLANE_TILE

    def body(c, carry):
        tiles = [load_rows(pl.multiple_of(blk * plan.n2 + c * g, g), g) for blk in range(plan.nb)]
        rhs = jnp.concatenate(tiles, axis=0).astype(BF16)
        a = jnp.dot(w1, rhs, preferred_element_type=F32)
        for k in range(plan.pa):
            _slab_store_block(a_ref, pl.multiple_of(k * plan.n2 + c * g, g), g, a[k * g:(k + 1) * g])
        return carry

    lax.fori_loop(0, plan.n2 // g, body, 0, unroll=True)


def _fft_stage2(plan, a_ref, w3_ref, k1):
    n2 = plan.n2
    aligned = (lambda r: r) if isinstance(k1, int) else (lambda r: pl.multiple_of(r, n2))
    re = _slab_rows(a_ref, aligned(k1 * n2), n2)
    im = _slab_rows(a_ref, aligned((plan.h1 + k1) * n2), n2)
    rhs = jnp.concatenate([re, im], axis=0).astype(BF16)
    return jnp.dot(w3_ref[k1], rhs, preferred_element_type=F32)


def _spectrum_kernel(plan, ff_ref, fb_ref, w1_ref, w3_ref, o_ref, af_ref, ab_ref):
    n2 = plan.n2
    _fft_stage1(plan, lambda start, size: ff_ref[pl.ds(start, size), :], af_ref, w1_ref)
    _fft_stage1(plan, lambda start, size: fb_ref[pl.ds(start, size), :], ab_ref, w1_ref)
    hb0 = fb_ref[0:1, :]

    def body(k1, carry):
        xf = _fft_stage2(plan, af_ref, w3_ref, k1)
        xb = _fft_stage2(plan, ab_ref, w3_ref, k1)
        o_ref[k1, 0:n2, :] = (xf[:n2] + xb[:n2] - hb0).astype(o_ref.dtype)
        o_ref[k1, n2:2 * n2, :] = (xf[n2:] - xb[n2:]).astype(o_ref.dtype)
        return carry

    lax.fori_loop(0, plan.h1, body, 0, unroll=11)


def _filter_spectrum(plan, filt, ct):
    seq_len, c2 = filt.shape
    c = c2 // 2
    assert c % ct == 0 and ct % LANES == 0
    nct = c // ct
    nslab = ct // LANES
    n2, h1, pa = plan.n2, plan.h1, plan.pa
    return pl.pallas_call(
        functools.partial(_spectrum_kernel, plan),
        grid=(nct,),
        in_specs=[
            pl.BlockSpec((seq_len, ct), lambda j: (0, j)),
            pl.BlockSpec((seq_len, ct), lambda j: (0, nct + j)),
            _const_spec(plan.w1.shape),
            _const_spec(plan.w3.shape),
        ],
        out_specs=pl.BlockSpec((h1, 2 * n2, ct), lambda j: (0, 0, j)),
        out_shape=jax.ShapeDtypeStruct((h1, 2 * n2, c), BF16),
        scratch_shapes=[
            pltpu.VMEM((nslab, n2 * pa, LANES), F32),
            pltpu.VMEM((nslab, n2 * pa, LANES), F32),
        ],
        compiler_params=_cparams(1),
        name="filter_spectrum",
    )(filt, filt, plan.w1, plan.w3)


def _hyena_kernel(plan, x0_ref, uin_ref, skip_ref, ks_ref, w1_ref, w3_ref, w3t_ref, mi_ref, o_ref,
                  u_ref, a_ref, b_ref):
    seq_len, n2, nb, h1 = plan.seq_len, plan.n2, plan.nb, plan.h1
    rows = COPY_ROWS
    g = F32_SUBLANE_TILE

    def load_u(c, carry):
        r0 = pl.multiple_of(c * rows, rows)
        _slab_store_block(u_ref, r0, rows, uin_ref[pl.ds(r0, rows), :].astype(F32))
        return carry

    lax.fori_loop(0, seq_len // rows, load_u, 0)

    _fft_stage1(plan, functools.partial(_slab_rows, u_ref), a_ref, w1_ref)

    for k1 in range(h1):
        x = _fft_stage2(plan, a_ref, w3_ref, k1)
        k = ks_ref[k1].astype(F32)
        xr, xi, kr, ki = x[:n2], x[n2:], k[:n2], k[n2:]
        y = jnp.concatenate([xr * kr - xi * ki, xr * ki + xi * kr], axis=0).astype(BF16)
        bm = jnp.dot(w3t_ref[k1], y, preferred_element_type=F32)
        _slab_store_block(b_ref, k1 * n2, n2, bm[:n2])
        im_block = plan.dump_block if k1 in (0, h1 - 1) else h1 - 1 + k1
        _slab_store_block(b_ref, im_block * n2, n2, bm[n2:])

    mi = mi_ref[...]
    skip = skip_ref[...]

    def last_stage(c, carry):
        tiles = [_slab_rows(b_ref, pl.multiple_of(q * n2 + c * g, g), g) for q in range(plan.nq)]
        rhs = jnp.concatenate(tiles, axis=0).astype(BF16)
        y = jnp.dot(mi, rhs, preferred_element_type=F32)
        for blk in range(nb):
            start = pl.multiple_of(blk * n2 + c * g, g)
            ucur = _slab_rows(u_ref, start, g)
            _slab_store_block(u_ref, start, g, y[blk * g:(blk + 1) * g] + skip * ucur)
        return carry

    lax.fori_loop(0, n2 // g, last_stage, 0, unroll=True)

    def gate_out(c, carry):
        r0 = pl.multiple_of(c * rows, rows)
        x0c = x0_ref[pl.ds(r0, rows), :].astype(F32)
        o_ref[pl.ds(r0, rows), :] = (x0c * _slab_rows(u_ref, r0, rows)).astype(o_ref.dtype)
        return carry

    lax.fori_loop(0, seq_len // rows, gate_out, 0)


def _hyena_operator(plan, proj3, skip, kspec, ct):
    bsz, seq_len, _ = proj3.shape
    c = skip.shape[-1]
    assert c % ct == 0 and ct % LANES == 0 and seq_len % COPY_ROWS == 0
    nct = c // ct
    nslab = ct // LANES
    n2, h1, pa = plan.n2, plan.h1, plan.pa
    part = lambda k: pl.BlockSpec((None, seq_len, ct), lambda j, b, k=k: (b, 0, k * nct + j))
    return pl.pallas_call(
        functools.partial(_hyena_kernel, plan),
        grid=(nct, bsz),
        in_specs=[
            part(0), part(1),
            pl.BlockSpec((1, ct), lambda j, b: (0, j)),
            pl.BlockSpec((h1, 2 * n2, ct), lambda j, b: (0, 0, j)),
            _const_spec(plan.w1.shape),
            _const_spec(plan.w3.shape),
            _const_spec(plan.w3t.shape),
            _const_spec(plan.mi.shape),
        ],
        out_specs=pl.BlockSpec((None, seq_len, ct), lambda j, b: (b, 0, j)),
        out_shape=jax.ShapeDtypeStruct((bsz, seq_len, c), BF16),
        scratch_shapes=[
            pltpu.VMEM((nslab, seq_len, LANES), F32),
            pltpu.VMEM((nslab, n2 * pa, LANES), F32),
            pltpu.VMEM((nslab, (plan.nq + 1) * n2, LANES), F32),
        ],
        compiler_params=_cparams(2),
        name="hyena_operator",
    )(proj3, proj3, skip.reshape(1, -1), kspec, plan.w1, plan.w3, plan.w3t, plan.mi)


def _merge_kernel(us_ref, vs_ref, ga_ref, gb_ref, ya_ref, x_ref, sg_ref, sw_ref, sbt_ref,
                  pa_ref, pb_ref, wo_ref, o_ref, yb_ref):
    heads, chunk, _ = sw_ref.shape
    tm, width = vs_ref.shape
    hd = width // heads
    ts = MERGE_SUB_ROWS
    for t0 in range(0, tm, ts):
        rows = slice(t0, t0 + ts)
        v = vs_ref[rows, :].astype(F32)
        ms = jnp.mean(v * v, axis=-1, keepdims=True)
        vn_all = (v * lax.rsqrt(ms + EPS) * sg_ref[...]).astype(BF16)
        for r in range(ts // chunk):
            rs = slice(t0 + r * chunk, t0 + (r + 1) * chunk)
            for g in range(heads):
                cs = slice(g * hd, (g + 1) * hd)
                s = (jnp.dot(sw_ref[g].astype(BF16), vn_all[r * chunk:(r + 1) * chunk, cs],
                             preferred_element_type=F32) + sbt_ref[:, g:g + 1])
                yb_ref[rs, cs] = (us_ref[rs, cs].astype(F32) * s).astype(BF16)
        pa = jnp.dot(ya_ref[rows, :], pa_ref[...].astype(BF16), preferred_element_type=F32)
        pb = jnp.dot(yb_ref[rows, :], pb_ref[...].astype(BF16), preferred_element_type=F32)
        merged = ga_ref[rows, :].astype(F32) * pa + gb_ref[rows, :].astype(F32) * pb
        o_ref[rows, :] = x_ref[rows, :] + jnp.dot(merged.astype(BF16), wo_ref[...].astype(BF16),
                                                  preferred_element_type=F32)


def _merge(proj, ya, x2d, sgu_g, sgu_w, sgu_b, pa_w, pb_w, wo_w, hy_cols, tm):
    t, d = x2d.shape
    width = sgu_g.shape[-1]
    heads, chunk, _ = sgu_w.shape
    assert t % tm == 0 and tm % MERGE_SUB_ROWS == 0 and MERGE_SUB_ROWS % chunk == 0
    assert hy_cols % width == 0 and ya.shape[1] == width
    base = hy_cols // width
    col = lambda k: pl.BlockSpec((tm, width), lambda i, k=k: (i, base + k))
    sbt = sgu_b.T
    return pl.pallas_call(
        _merge_kernel,
        grid=(t // tm,),
        in_specs=[
            col(0), col(1), col(2), col(3),
            pl.BlockSpec((tm, width), lambda i: (i, 0)),
            pl.BlockSpec((tm, d), lambda i: (i, 0)),
            _const_spec((1, width)),
            _const_spec(sgu_w.shape),
            _const_spec(sbt.shape),
            _const_spec(pa_w.shape),
            _const_spec(pb_w.shape),
            _const_spec(wo_w.shape),
        ],
        out_specs=pl.BlockSpec((tm, d), lambda i: (i, 0)),
        out_shape=jax.ShapeDtypeStruct((t, d), F32),
        scratch_shapes=[pltpu.VMEM((tm, width), BF16)],
        compiler_params=_cparams(1),
        name="sgu_merge",
    )(proj, proj, proj, proj, ya, x2d, sgu_g.reshape(1, -1), sgu_w, sbt, pa_w, pb_w, wo_w)


FFN_CHUNK = 256
FFN_SUB_ROWS = 512


def _ffn_kernel(xm_ref, xp_ref, xn_ref, g_ref, wup_ref, cw_ref, cb_ref, wdn_ref, fg_ref, o_ref,
                hs_ref, act_ref, stage_ref, *, tiles_per_seq, final_norm):
    tm, _ = xm_ref.shape
    hidden = wdn_ref.shape[0]
    halo = ROW_HALO
    i = pl.program_id(0)
    g = g_ref[...]

    def norm(xv):
        ms = jnp.mean(xv * xv, axis=-1, keepdims=True)
        return xv * lax.rsqrt(ms + EPS) * g

    at_start = (i % tiles_per_seq) == 0
    at_end = (i % tiles_per_seq) == tiles_per_seq - 1
    hs_ref[0:halo, :] = jnp.where(at_start, 0.0, norm(xp_ref[...])).astype(BF16)
    hs_ref[halo + tm:, :] = jnp.where(at_end, 0.0, norm(xn_ref[...])).astype(BF16)

    ts = FFN_SUB_ROWS
    ext = ts + 2 * halo
    for s in range(tm // ts):
        rs = slice(s * ts, (s + 1) * ts)
        hs_ref[halo + s * ts:halo + (s + 1) * ts, :] = norm(xm_ref[rs, :]).astype(BF16)
    for s in range(tm // ts):
        rs = slice(s * ts, (s + 1) * ts)
        h_ext = hs_ref[s * ts:s * ts + ext, :]
        h_main = hs_ref[halo + s * ts:halo + (s + 1) * ts, :]
        for k in range(hidden // FFN_CHUNK):
            cs = slice(k * FFN_CHUNK, (k + 1) * FFN_CHUNK)
            gs = slice(hidden + k * FFN_CHUNK, hidden + (k + 1) * FFN_CHUNK)
            a_ext = jnp.dot(h_ext, wup_ref[:, cs].astype(BF16), preferred_element_type=F32)
            gate = jnp.dot(h_main, wup_ref[:, gs].astype(BF16), preferred_element_type=F32)
            stage = stage_ref.at[k % 2]
            _slab_store_block(stage, 0, ext, a_ext)
            up, mid, dn = (_slab_rows(stage, halo + shift, ts) for shift in (-1, 0, 1))
            w = cw_ref[:, cs]
            c = w[0:1] * up + w[1:2] * mid + w[2:3] * dn + cb_ref[:, cs]
            act_ref[rs, cs] = (c / (1.0 + jnp.exp(-c)) * gate).astype(BF16)
        y = xm_ref[rs, :] + jnp.dot(act_ref[rs, :], wdn_ref[...].astype(BF16),
                                    preferred_element_type=F32)
        if final_norm:
            ms = jnp.mean(y * y, axis=-1, keepdims=True)
            y = y * lax.rsqrt(ms + EPS) * fg_ref[...]
        o_ref[rs, :] = y


def _ffn(x2d, seq_len, g, wup, conv_w, conv_b, wdn, final_g, final_norm, tm):
    t, d = x2d.shape
    hidden = wdn.shape[0]
    halo = ROW_HALO
    assert t % tm == 0 and seq_len % tm == 0 and tm % FFN_SUB_ROWS == 0 and FFN_SUB_ROWS % halo == 0
    assert hidden % FFN_CHUNK == 0
    per = tm // halo
    last = t // halo - 1
    kern = functools.partial(_ffn_kernel, tiles_per_seq=seq_len // tm, final_norm=final_norm)
    return pl.pallas_call(
        kern,
        grid=(t // tm,),
        in_specs=[
            pl.BlockSpec((tm, d), lambda i: (i, 0)),
            pl.BlockSpec((halo, d), lambda i: (jnp.maximum(i * per - 1, 0), 0)),
            pl.BlockSpec((halo, d), lambda i: (jnp.minimum((i + 1) * per, last), 0)),
            _const_spec((1, d)),
            _const_spec(wup.shape),
            _const_spec(conv_w.shape),
            _const_spec((1, hidden)),
            _const_spec(wdn.shape),
            _const_spec((1, d)),
        ],
        out_specs=pl.BlockSpec((tm, d), lambda i: (i, 0)),
        out_shape=jax.ShapeDtypeStruct((t, d), F32),
        scratch_shapes=[pltpu.VMEM((tm + 2 * halo, d), BF16), pltpu.VMEM((tm, hidden), BF16),
                        pltpu.VMEM((2, FFN_CHUNK // LANES, FFN_SUB_ROWS + 2 * halo, LANES), F32)],
        compiler_params=_cparams(1),
        name="ffn",
    )(x2d, x2d, x2d, g.reshape(1, -1), wup, conv_w, conv_b.reshape(1, -1), wdn,
      final_g.reshape(1, -1))


def kernel(x, norm1_g, w_in, hy_conv_w, hy_conv_b, filt_w1, filt_b1, filt_w2, filt_b2, filt_w3, filt_b3, filt_freq, filt_w4, hy_decay, hy_skip, sgu_norm_g, sgu_w, sgu_b, w_proj_hyena, w_proj_sgu, w_out, norm2_g, w_up, ffn_conv_w, ffn_conv_b, w_down, final_g):
    bsz, seq_len, d = x.shape
    depth = norm1_g.shape[0]
    hw = hy_skip.shape[-1]
    sw = sgu_norm_g.shape[-1]
    plan = _FftPlan(seq_len)
    x2d = x.reshape(bsz * seq_len, d)
    for i in range(depth):
        proj = _in_projection(x2d, seq_len, norm1_g[i].reshape(1, -1), w_in[i], hy_conv_w[i],
                              hy_conv_b[i], hw=hw, n_gelu_cols=2 * sw, tm=512, tn=512)
        filt = _implicit_filters(seq_len, filt_w1[i], filt_b1[i], filt_w2[i], filt_b2[i], filt_w3[i],
                                 filt_b3[i], filt_freq[i], filt_w4[i], hy_decay[i])
        kspec = _filter_spectrum(plan, filt, HYENA_CH)
        ya = _hyena_operator(plan, proj.reshape(bsz, seq_len, -1), hy_skip[i], kspec, HYENA_CH)
        x2d = _merge(proj, ya.reshape(bsz * seq_len, hw), x2d, sgu_norm_g[i], sgu_w[i],
                     sgu_b[i], w_proj_hyena[i], w_proj_sgu[i], w_out[i], hy_cols=2 * hw, tm=1024)
        x2d = _ffn(x2d, seq_len, norm2_g[i], w_up[i], ffn_conv_w[i], ffn_conv_b[i],
                   w_down[i], final_g, final_norm=(i == depth - 1), tm=1024)
    return x2d.reshape(bsz, seq_len, d)
```

```python
import functools
import math

import numpy as np
import jax
import jax.numpy as jnp
from jax import lax
from jax.experimental import pallas as pl
from jax.experimental.pallas import tpu as pltpu

EPS = 1e-6
F32 = jnp.float32
BF16 = jnp.bfloat16

LANES = 128
F32_SUBLANE_TILE = 8
BF16_SUBLANE_TILE = 16
VMEM_LIMIT_BYTES = 60 * 1024 * 1024

ROW_HALO = BF16_SUBLANE_TILE
FFT_INNER = 128
COPY_ROWS = 256
HYENA_CH = 256
MERGE_SUB_ROWS = 512


def _cparams(n_axes):
    return pltpu.CompilerParams(
        dimension_semantics=("arbitrary",) * n_axes, vmem_limit_bytes=VMEM_LIMIT_BYTES)


def _const_spec(shape):
    nd = len(shape)
    return pl.BlockSpec(shape, lambda *_: (0,) * nd, pipeline_mode=pl.Buffered(1))


def _inproj_kernel(xm_ref, xp_ref, xn_ref, g_ref, w_ref, cw_ref, cb_ref, o_ref, hn_ref, stage_ref, *,
                   hw, n_gelu_cols, tn, tiles_per_seq):
    tm = xm_ref.shape[0]
    halo = ROW_HALO
    ext = tm + 2 * halo
    i = pl.program_id(0)
    g = g_ref[...]

    def norm(xv):
        ms = jnp.mean(xv * xv, axis=-1, keepdims=True)
        return xv * lax.rsqrt(ms + EPS) * g

    at_start = (i % tiles_per_seq) == 0
    at_end = (i % tiles_per_seq) == tiles_per_seq - 1
    hn_ref[0:halo, :] = jnp.where(at_start, 0.0, norm(xp_ref[...])).astype(BF16)
    hn_ref[halo:halo + tm, :] = norm(xm_ref[...]).astype(BF16)
    hn_ref[halo + tm:, :] = jnp.where(at_end, 0.0, norm(xn_ref[...])).astype(BF16)

    def conv_cols(col0, buf):
        cs = slice(col0, col0 + tn)
        acc = jnp.dot(hn_ref[...], w_ref[:, cs].astype(BF16), preferred_element_type=F32)
        stage = stage_ref.at[buf]
        _slab_store_block(stage, 0, ext, acc)
        up, mid, dn = (_slab_rows(stage, halo + shift, tm) for shift in (-1, 0, 1))
        w = cw_ref[:, cs]
        return w[0:1] * up + w[1:2] * mid + w[2:3] * dn + cb_ref[:, cs]

    for k in range(hw // tn):
        o_ref[:, k * tn:(k + 1) * tn] = conv_cols(k * tn, 0).astype(o_ref.dtype)
    for k in range(hw // tn):
        u = conv_cols(hw + k * tn, 0) * conv_cols(2 * hw + k * tn, 1)
        o_ref[:, hw + k * tn:hw + (k + 1) * tn] = u.astype(o_ref.dtype)
    for k in range((w_ref.shape[1] - 3 * hw) // tn):
        cs = slice(3 * hw + k * tn, 3 * hw + (k + 1) * tn)
        acc = jnp.dot(hn_ref[halo:halo + tm, :], w_ref[:, cs].astype(BF16), preferred_element_type=F32)
        if k * tn < n_gelu_cols:
            out = 0.5 * acc * (1.0 + lax.erf(acc * math.sqrt(0.5)))
        else:
            out = 1.0 / (1.0 + jnp.exp(-acc))
        o_ref[:, 2 * hw + k * tn:2 * hw + (k + 1) * tn] = out.astype(o_ref.dtype)


def _in_projection(x2d, seq_len, g, w_in, conv_w, conv_b, hw, n_gelu_cols, tm, tn):
    t, d = x2d.shape
    n = w_in.shape[1]
    halo = ROW_HALO
    assert t % tm == 0 and seq_len % tm == 0 and tm % halo == 0 and tn % LANES == 0
    assert hw % tn == 0 and n_gelu_cols % tn == 0 and (n - 3 * hw) % tn == 0
    per = tm // halo
    last = t // halo - 1
    kern = functools.partial(_inproj_kernel, hw=hw, n_gelu_cols=n_gelu_cols, tn=tn,
                             tiles_per_seq=seq_len // tm)
    return pl.pallas_call(
        kern,
        grid=(t // tm,),
        in_specs=[
            pl.BlockSpec((tm, d), lambda i: (i, 0)),
            pl.BlockSpec((halo, d), lambda i: (jnp.maximum(i * per - 1, 0), 0)),
            pl.BlockSpec((halo, d), lambda i: (jnp.minimum((i + 1) * per, last), 0)),
            _const_spec((1, d)),
            _const_spec(w_in.shape),
            _const_spec(conv_w.shape),
            _const_spec((1, conv_b.shape[-1])),
        ],
        out_specs=pl.BlockSpec((tm, n - hw), lambda i: (i, 0)),
        out_shape=jax.ShapeDtypeStruct((t, n - hw), BF16),
        scratch_shapes=[pltpu.VMEM((tm + 2 * halo, d), BF16),
                        pltpu.VMEM((2, tn // LANES, tm + 2 * halo, LANES), F32)],
        compiler_params=_cparams(1),
        name="in_projection",
    )(x2d, x2d, x2d, g, w_in, conv_w, conv_b.reshape(1, -1))


def _filter_kernel(w1t_ref, w1c_ref, w1s_ref, b1_ref, w2_ref, b2_ref, w3_ref, b3_ref, fr_ref,
                   w4_ref, dec_ref, o_ref, *, seq_len, bands):
    tl = o_ref.shape[0]
    hi = lax.Precision.HIGHEST
    dot = functools.partial(jnp.dot, preferred_element_type=F32, precision=hi)
    r0 = pl.program_id(0) * tl
    pos = (r0 + lax.broadcasted_iota(jnp.int32, (1, tl), 1)).astype(F32)
    t = pos / float(seq_len - 1)
    band_step = (bands - 1 - 1e-4) / (bands - 1)
    band = 1e-4 + band_step * lax.broadcasted_iota(jnp.int32, (bands, 1), 0).astype(F32)
    phase = ((2.0 * math.pi / seq_len) * pos) * band
    a = fr_ref[...]
    z1 = w1t_ref[...] * t + dot(w1c_ref[...], jnp.cos(phase)) - dot(w1s_ref[...], jnp.sin(phase))
    h = jnp.sin(a * (z1 + b1_ref[...]))
    h = jnp.sin(a * (dot(w2_ref[...], h) + b2_ref[...]))
    h = jnp.sin(a * (dot(w3_ref[...], h) + b3_ref[...]))
    def split(v):
        v_hi = v.astype(BF16)
        return v_hi, (v - v_hi.astype(F32)).astype(BF16)

    h_hi, h_lo = split(h)
    w_hi, w_lo = split(w4_ref[...])
    lhs = jnp.concatenate([h_hi, h_lo, h_hi, h_lo], axis=0)
    rhs = jnp.concatenate([w_hi, w_hi, w_lo, w_lo], axis=0)
    f = lax.dot_general(lhs, rhs, (((0,), (0,)), ((), ())), preferred_element_type=F32)
    t_col = (r0 + lax.broadcasted_iota(jnp.int32, (tl, 1), 0)).astype(F32) / float(seq_len - 1)
    o_ref[...] = f * jnp.exp(-t_col * jnp.abs(dec_ref[...]))


def _implicit_filters(seq_len, w1, b1, w2, b2, w3, b3, freq, w4, decay, tl=512):
    bands = (w1.shape[0] - 1) // 2
    n_out = w4.shape[1]
    assert seq_len % tl == 0
    col = lambda v: v.reshape(-1, 1)
    args = (w1[0:1].T, w1[1:1 + bands].T, w1[1 + bands:].T, col(b1), w2.T, col(b2), w3.T, col(b3),
            col(freq), w4, decay.reshape(1, -1))
    full = lambda a: pl.BlockSpec(a.shape, lambda i: (0, 0))
    kern = functools.partial(_filter_kernel, seq_len=seq_len, bands=bands)
    return pl.pallas_call(
        kern,
        grid=(seq_len // tl,),
        in_specs=[full(a) for a in args],
        out_specs=pl.BlockSpec((tl, n_out), lambda i: (i, 0)),
        out_shape=jax.ShapeDtypeStruct((seq_len, n_out), F32),
        compiler_params=_cparams(1),
        name="implicit_filter",
    )(*args)


class _FftPlan:
    def __init__(self, seq_len):
        n2 = FFT_INNER
        assert seq_len % n2 == 0
        self.seq_len = seq_len
        self.n = 2 * seq_len
        self.n2 = n2
        self.n1 = self.n // n2
        self.nb = seq_len // n2
        self.h1 = self.n1 // 2 + 1
        self.pa = 2 * self.h1
        self.nq = self.n1
        self.dump_block = self.n1
        assert self.nb % BF16_SUBLANE_TILE == 0
        n1, nb, h1, n = self.n1, self.nb, self.h1, self.n
        k1 = np.arange(h1)[:, None]
        ph = 2.0 * np.pi * k1 * np.arange(nb)[None, :] / n1
        w1 = np.zeros((self.pa, nb))
        w1[:h1] = np.cos(ph)
        w1[h1:2 * h1] = -np.sin(ph)
        k2 = np.arange(n2)[:, None]
        m2 = np.arange(n2)[None, :]
        w3 = np.zeros((h1, 2 * n2, 2 * n2))
        for k in range(h1):
            th = 2.0 * np.pi * (((k + n1 * k2) * m2) % n) / n
            er, ei = np.cos(th), -np.sin(th)
            w3[k] = np.block([[er, -ei], [ei, er]])
        ck = np.full(h1, 2.0)
        ck[0] = 1.0
        ck[-1] = 1.0
        phi = 2.0 * np.pi * np.arange(nb)[:, None] * np.arange(h1)[None, :] / n1
        mi = np.zeros((nb, self.nq))
        mi[:, :h1] = ck * np.cos(phi) / n
        mi[:, h1:] = (-ck * np.sin(phi) / n)[:, 1:h1 - 1]
        g = F32_SUBLANE_TILE
        w1k = np.einsum("kn,ij->kinj", w1, np.eye(g)).reshape(self.pa * g, nb * g)
        self.w1 = jnp.asarray(w1k, F32).astype(BF16)
        self.w3 = jnp.asarray(w3, F32).astype(BF16)
        self.w3t = jnp.asarray(np.transpose(w3, (0, 2, 1)), F32).astype(BF16)
        mik = np.einsum("nq,ij->niqj", mi, np.eye(g)).reshape(nb * g, self.nq * g)
        self.mi = jnp.asarray(mik, F32).astype(BF16)


def _slab_store_block(ref, start, size, val):
    for s in range(ref.shape[0]):
        ref[s, pl.ds(start, size), :] = val[:, s * LANES:(s + 1) * LANES]


def _slab_rows(ref, start, size):
    return jnp.concatenate([ref[s, pl.ds(start, size), :] for s in range(ref.shape[0])], axis=1)


def _fft_stage1(plan, load_rows, a_ref, w1_ref):
    w1 = w1_ref[...]
    g = F32_SUBLANE_TILE

    def body(c, carry):
        tiles = [load_rows(pl.multiple_of(blk * plan.n2 + c * g, g), g) for blk in range(plan.nb)]
        rhs = jnp.concatenate(tiles, axis=0).astype(BF16)
        a = jnp.dot(w1, rhs, preferred_element_type=F32)
        for k in range(plan.pa):
            _slab_store_block(a_ref, pl.multiple_of(k * plan.n2 + c * g, g), g, a[k * g:(k + 1) * g])
        return carry

    lax.fori_loop(0, plan.n2 // g, body, 0, unroll=True)


def _fft_stage2(plan, a_ref, w3_ref, k1):
    n2 = plan.n2
    aligned = (lambda r: r) if isinstance(k1, int) else (lambda r: pl.multiple_of(r, n2))
    re = _slab_rows(a_ref, aligned(k1 * n2), n2)
    im = _slab_rows(a_ref, aligned((plan.h1 + k1) * n2), n2)
    rhs = jnp.concatenate([re, im], axis=0).astype(BF16)
    return jnp.dot(w3_ref[k1], rhs, preferred_element_type=F32)


def _spectrum_kernel(plan, ff_ref, fb_ref, w1_ref, w3_ref, o_ref, af_ref, ab_ref):
    n2 = plan.n2
    _fft_stage1(plan, lambda start, size: ff_ref[pl.ds(start, size), :], af_ref, w1_ref)
    _fft_stage1(plan, lambda start, size: fb_ref[pl.ds(start, size), :], ab_ref, w1_ref)
    hb0 = fb_ref[0:1, :]

    def body(k1, carry):
        xf = _fft_stage2(plan, af_ref, w3_ref, k1)
        xb = _fft_stage2(plan, ab_ref, w3_ref, k1)
        o_ref[k1, 0:n2, :] = (xf[:n2] + xb[:n2] - hb0).astype(o_ref.dtype)
        o_ref[k1, n2:2 * n2, :] = (xf[n2:] - xb[n2:]).astype(o_ref.dtype)
        return carry

    lax.fori_loop(0, plan.h1, body, 0, unroll=True)


def _filter_spectrum(plan, filt, ct):
    seq_len, c2 = filt.shape
    c = c2 // 2
    assert c % ct == 0 and ct % LANES == 0
    nct = c // ct
    nslab = ct // LANES
    n2, h1, pa = plan.n2, plan.h1, plan.pa
    return pl.pallas_call(
        functools.partial(_spectrum_kernel, plan),
        grid=(nct,),
        in_specs=[
            pl.BlockSpec((seq_len, ct), lambda j: (0, j)),
            pl.BlockSpec((seq_len, ct), lambda j: (0, nct + j)),
            _const_spec(plan.w1.shape),
            _const_spec(plan.w3.shape),
        ],
        out_specs=pl.BlockSpec((h1, 2 * n2, ct), lambda j: (0, 0, j)),
        out_shape=jax.ShapeDtypeStruct((h1, 2 * n2, c), BF16),
        scratch_shapes=[
            pltpu.VMEM((nslab, n2 * pa, LANES), F32),
            pltpu.VMEM((nslab, n2 * pa, LANES), F32),
        ],
        compiler_params=_cparams(1),
        name="filter_spectrum",
    )(filt, filt, plan.w1, plan.w3)


def _hyena_kernel(plan, x0_ref, uin_ref, skip_ref, ks_ref, w1_ref, w3_ref, w3t_ref, mi_ref, o_ref,
                  u_ref, a_ref, b_ref):
    seq_len, n2, nb, h1 = plan.seq_len, plan.n2, plan.nb, plan.h1
    rows = COPY_ROWS
    g = F32_SUBLANE_TILE

    def load_u(c, carry):
        r0 = pl.multiple_of(c * rows, rows)
        _slab_store_block(u_ref, r0, rows, uin_ref[pl.ds(r0, rows), :].astype(F32))
        return carry

    lax.fori_loop(0, seq_len // rows, load_u, 0)

    _fft_stage1(plan, functools.partial(_slab_rows, u_ref), a_ref, w1_ref)

    for k1 in range(h1):
        x = _fft_stage2(plan, a_ref, w3_ref, k1)
        k = ks_ref[k1].astype(F32)
        xr, xi, kr, ki = x[:n2], x[n2:], k[:n2], k[n2:]
        y = jnp.concatenate([xr * kr - xi * ki, xr * ki + xi * kr], axis=0).astype(BF16)
        bm = jnp.dot(w3t_ref[k1], y, preferred_element_type=F32)
        _slab_store_block(b_ref, k1 * n2, n2, bm[:n2])
        im_block = plan.dump_block if k1 in (0, h1 - 1) else h1 - 1 + k1
        _slab_store_block(b_ref, im_block * n2, n2, bm[n2:])

    mi = mi_ref[...]
    skip = skip_ref[...]

    def last_stage(c, carry):
        tiles = [_slab_rows(b_ref, pl.multiple_of(q * n2 + c * g, g), g) for q in range(plan.nq)]
        rhs = jnp.concatenate(tiles, axis=0).astype(BF16)
        y = jnp.dot(mi, rhs, preferred_element_type=F32)
        for blk in range(nb):
            start = pl.multiple_of(blk * n2 + c * g, g)
            ucur = _slab_rows(u_ref, start, g)
            _slab_store_block(u_ref, start, g, y[blk * g:(blk + 1) * g] + skip * ucur)
        return carry

    lax.fori_loop(0, n2 // g, last_stage, 0, unroll=True)

    def gate_out(c, carry):
        r0 = pl.multiple_of(c * rows, rows)
        x0c = x0_ref[pl.ds(r0, rows), :].astype(F32)
        o_ref[pl.ds(r0, rows), :] = (x0c * _slab_rows(u_ref, r0, rows)).astype(o_ref.dtype)
        return carry

    lax.fori_loop(0, seq_len // rows, gate_out, 0)


def _hyena_operator(plan, proj3, skip, kspec, ct):
    bsz, seq_len, _ = proj3.shape
    c = skip.shape[-1]
    assert c % ct == 0 and ct % LANES == 0 and seq_len % COPY_ROWS == 0
    nct = c // ct
    nslab = ct // LANES
    n2, h1, pa = plan.n2, plan.h1, plan.pa
    part = lambda k: pl.BlockSpec((None, seq_len, ct), lambda j, b, k=k: (b, 0, k * nct + j))
    return pl.pallas_call(
        functools.partial(_hyena_kernel, plan),
        grid=(nct, bsz),
        in_specs=[
            part(0), part(1),
            pl.BlockSpec((1, ct), lambda j, b: (0, j)),
            pl.BlockSpec((h1, 2 * n2, ct), lambda j, b: (0, 0, j)),
            _const_spec(plan.w1.shape),
            _const_spec(plan.w3.shape),
            _const_spec(plan.w3t.shape),
            _const_spec(plan.mi.shape),
        ],
        out_specs=pl.BlockSpec((None, seq_len, ct), lambda j, b: (b, 0, j)),
        out_shape=jax.ShapeDtypeStruct((bsz, seq_len, c), BF16),
        scratch_shapes=[
            pltpu.VMEM((nslab, seq_len, LANES), F32),
            pltpu.VMEM((nslab, n2 * pa, LANES), F32),
            pltpu.VMEM((nslab, (plan.nq + 1) * n2, LANES), F32),
        ],
        compiler_params=_cparams(2),
        name="hyena_operator",
    )(proj3, proj3, skip.reshape(1, -1), kspec, plan.w1, plan.w3, plan.w3t, plan.mi)


def _merge_kernel(us_ref, vs_ref, ga_ref, gb_ref, ya_ref, x_ref, sg_ref, sw_ref, sbt_ref,
                  pa_ref, pb_ref, wo_ref, o_ref, yb_ref):
    heads, chunk, _ = sw_ref.shape
    tm, width = vs_ref.shape
    hd = width // heads
    ts = MERGE_SUB_ROWS
    for t0 in range(0, tm, ts):
        rows = slice(t0, t0 + ts)
        v = vs_ref[rows, :].astype(F32)
        ms = jnp.mean(v * v, axis=-1, keepdims=True)
        vn_all = (v * lax.rsqrt(ms + EPS) * sg_ref[...]).astype(BF16)
        for r in range(ts // chunk):
            rs = slice(t0 + r * chunk, t0 + (r + 1) * chunk)
            for g in range(heads):
                cs = slice(g * hd, (g + 1) * hd)
                s = (jnp.dot(sw_ref[g].astype(BF16), vn_all[r * chunk:(r + 1) * chunk, cs],
                             preferred_element_type=F32) + sbt_ref[:, g:g + 1])
                yb_ref[rs, cs] = (us_ref[rs, cs].astype(F32) * s).astype(BF16)
        pa = jnp.dot(ya_ref[rows, :], pa_ref[...].astype(BF16), preferred_element_type=F32)
        pb = jnp.dot(yb_ref[rows, :], pb_ref[...].astype(BF16), preferred_element_type=F32)
        merged = ga_ref[rows, :].astype(F32) * pa + gb_ref[rows, :].astype(F32) * pb
        o_ref[rows, :] = x_ref[rows, :] + jnp.dot(merged.astype(BF16), wo_ref[...].astype(BF16),
                                                  preferred_element_type=F32)


def _merge(proj, ya, x2d, sgu_g, sgu_w, sgu_b, pa_w, pb_w, wo_w, hy_cols, tm):
    t, d = x2d.shape
    width = sgu_g.shape[-1]
    heads, chunk, _ = sgu_w.shape
    assert t % tm == 0 and tm % MERGE_SUB_ROWS == 0 and MERGE_SUB_ROWS % chunk == 0
    assert hy_cols % width == 0 and ya.shape[1] == width
    base = hy_cols // width
    col = lambda k: pl.BlockSpec((tm, width), lambda i, k=k: (i, base + k))
    sbt = sgu_b.T
    return pl.pallas_call(
        _merge_kernel,
        grid=(t // tm,),
        in_specs=[
            col(0), col(1), col(2), col(3),
            pl.BlockSpec((tm, width), lambda i: (i, 0)),
            pl.BlockSpec((tm, d), lambda i: (i, 0)),
            _const_spec((1, width)),
            _const_spec(sgu_w.shape),
            _const_spec(sbt.shape),
            _const_spec(pa_w.shape),
            _const_spec(pb_w.shape),
            _const_spec(wo_w.shape),
        ],
        out_specs=pl.BlockSpec((tm, d), lambda i: (i, 0)),
        out_shape=jax.ShapeDtypeStruct((t, d), F32),
        scratch_shapes=[pltpu.VMEM((tm, width), BF16)],
        compiler_params=_cparams(1),
        name="sgu_merge",
    )(proj, proj, proj, proj, ya, x2d, sgu_g.reshape(1, -1), sgu_w, sbt, pa_w, pb_w, wo_w)


FFN_CHUNK = 256
FFN_SUB_ROWS = 512


def _ffn_kernel(xm_ref, xp_ref, xn_ref, g_ref, wup_ref, cw_ref, cb_ref, wdn_ref, fg_ref, o_ref,
                hs_ref, act_ref, stage_ref, *, tiles_per_seq, final_norm):
    tm, _ = xm_ref.shape
    hidden = wdn_ref.shape[0]
    halo = ROW_HALO
    i = pl.program_id(0)
    g = g_ref[...]

    def norm(xv):
        ms = jnp.mean(xv * xv, axis=-1, keepdims=True)
        return xv * lax.rsqrt(ms + EPS) * g

    at_start = (i % tiles_per_seq) == 0
    at_end = (i % tiles_per_seq) == tiles_per_seq - 1
    hs_ref[0:halo, :] = jnp.where(at_start, 0.0, norm(xp_ref[...])).astype(BF16)
    hs_ref[halo + tm:, :] = jnp.where(at_end, 0.0, norm(xn_ref[...])).astype(BF16)

    ts = FFN_SUB_ROWS
    ext = ts + 2 * halo
    for s in range(tm // ts):
        rs = slice(s * ts, (s + 1) * ts)
        hs_ref[halo + s * ts:halo + (s + 1) * ts, :] = norm(xm_ref[rs, :]).astype(BF16)
    for s in range(tm // ts):
        rs = slice(s * ts, (s + 1) * ts)
        h_ext = hs_ref[s * ts:s * ts + ext, :]
        h_main = hs_ref[halo + s * ts:halo + (s + 1) * ts, :]
        for k in range(hidden // FFN_CHUNK):
            cs = slice(k * FFN_CHUNK, (k + 1) * FFN_CHUNK)
            gs = slice(hidden + k * FFN_CHUNK, hidden + (k + 1) * FFN_CHUNK)
            a_ext = jnp.dot(h_ext, wup_ref[:, cs].astype(BF16), preferred_element_type=F32)
            gate = jnp.dot(h_main, wup_ref[:, gs].astype(BF16), preferred_element_type=F32)
            stage = stage_ref.at[k % 2]
            _slab_store_block(stage, 0, ext, a_ext)
            up, mid, dn = (_slab_rows(stage, halo + shift, ts) for shift in (-1, 0, 1))
            w = cw_ref[:, cs]
            c = w[0:1] * up + w[1:2] * mid + w[2:3] * dn + cb_ref[:, cs]
            act_ref[rs, cs] = (c / (1.0 + jnp.exp(-c)) * gate).astype(BF16)
        y = xm_ref[rs, :] + jnp.dot(act_ref[rs, :], wdn_ref[...].astype(BF16),
                                    preferred_element_type=F32)
        if final_norm:
            ms = jnp.mean(y * y, axis=-1, keepdims=True)
            y = y * lax.rsqrt(ms + EPS) * fg_ref[...]
        o_ref[rs, :] = y


def _ffn(x2d, seq_len, g, wup, conv_w, conv_b, wdn, final_g, final_norm, tm):
    t, d = x2d.shape
    hidden = wdn.shape[0]
    halo = ROW_HALO
    assert t % tm == 0 and seq_len % tm == 0 and tm % FFN_SUB_ROWS == 0 and FFN_SUB_ROWS % halo == 0
    assert hidden % FFN_CHUNK == 0
    per = tm // halo
    last = t // halo - 1
    kern = functools.partial(_ffn_kernel, tiles_per_seq=seq_len // tm, final_norm=final_norm)
    return pl.pallas_call(
        kern,
        grid=(t // tm,),
        in_specs=[
            pl.BlockSpec((tm, d), lambda i: (i, 0)),
            pl.BlockSpec((halo, d), lambda i: (jnp.maximum(i * per - 1, 0), 0)),
            pl.BlockSpec((halo, d), lambda i: (jnp.minimum((i + 1) * per, last), 0)),
            _const_spec((1, d)),
            _const_spec(wup.shape),
            _const_spec(conv_w.shape),
            _const_spec((1, hidden)),
            _const_spec(wdn.shape),
            _const_spec((1, d)),
        ],
        out_specs=pl.BlockSpec((tm, d), lambda i: (i, 0)),
        out_shape=jax.ShapeDtypeStruct((t, d), F32),
        scratch_shapes=[pltpu.VMEM((tm + 2 * halo, d), BF16), pltpu.VMEM((tm, hidden), BF16),
                        pltpu.VMEM((2, FFN_CHUNK // LANES, FFN_SUB_ROWS + 2 * halo, LANES), F32)],
        compiler_params=_cparams(1),
        name="ffn",
    )(x2d, x2d, x2d, g.reshape(1, -1), wup, conv_w, conv_b.reshape(1, -1), wdn,
      final_g.reshape(1, -1))


def kernel(x, norm1_g, w_in, hy_conv_w, hy_conv_b, filt_w1, filt_b1, filt_w2, filt_b2, filt_w3, filt_b3, filt_freq, filt_w4, hy_decay, hy_skip, sgu_norm_g, sgu_w, sgu_b, w_proj_hyena, w_proj_sgu, w_out, norm2_g, w_up, ffn_conv_w, ffn_conv_b, w_down, final_g):
    bsz, seq_len, d = x.shape
    depth = norm1_g.shape[0]
    hw = hy_skip.shape[-1]
    sw = sgu_norm_g.shape[-1]
    plan = _FftPlan(seq_len)
    x2d = x.reshape(bsz * seq_len, d)
    for i in range(depth):
        proj = _in_projection(x2d, seq_len, norm1_g[i].reshape(1, -1), w_in[i], hy_conv_w[i],
                              hy_conv_b[i], hw=hw, n_gelu_cols=2 * sw, tm=512, tn=512)
        filt = _implicit_filters(seq_len, filt_w1[i], filt_b1[i], filt_w2[i], filt_b2[i], filt_w3[i],
                                 filt_b3[i], filt_freq[i], filt_w4[i], hy_decay[i])
        kspec = _filter_spectrum(plan, filt, HYENA_CH)
        ya = _hyena_operator(plan, proj.reshape(bsz, seq_len, -1), hy_skip[i], kspec, HYENA_CH)
        x2d = _merge(proj, ya.reshape(bsz * seq_len, hw), x2d, sgu_norm_g[i], sgu_w[i],
                     sgu_b[i], w_proj_hyena[i], w_proj_sgu[i], w_out[i], hy_cols=2 * hw, tm=1024)
        x2d = _ffn(x2d, seq_len, norm2_g[i], w_up[i], ffn_conv_w[i], ffn_conv_b[i],
                   w_down[i], final_g, final_norm=(i == depth - 1), tm=1024)
    return x2d.reshape(bsz, seq_len, d)
```

```python
import functools
import math

import numpy as np
import jax
import jax.numpy as jnp
from jax import lax
from jax.experimental import pallas as pl
from jax.experimental.pallas import tpu as pltpu

EPS = 1e-6
F32 = jnp.float32
BF16 = jnp.bfloat16

LANES = 128
F32_SUBLANE_TILE = 8
BF16_SUBLANE_TILE = 16
VMEM_LIMIT_BYTES = 60 * 1024 * 1024

ROW_HALO = BF16_SUBLANE_TILE
FFT_INNER = 128
COPY_ROWS = 256
HYENA_CH = 256
MERGE_SUB_ROWS = 512


def _cparams(n_axes):
    return pltpu.CompilerParams(
        dimension_semantics=("arbitrary",) * n_axes, vmem_limit_bytes=VMEM_LIMIT_BYTES)


def _const_spec(shape):
    nd = len(shape)
    return pl.BlockSpec(shape, lambda *_: (0,) * nd, pipeline_mode=pl.Buffered(1))


def _inproj_kernel(xm_ref, xp_ref, xn_ref, g_ref, w_ref, cw_ref, cb_ref, o_ref, hn_ref, stage_ref, *,
                   hw, n_gelu_cols, tn, tiles_per_seq):
    tm = xm_ref.shape[0]
    halo = ROW_HALO
    ext = tm + 2 * halo
    i = pl.program_id(0)
    g = g_ref[...]

    def norm(xv):
        ms = jnp.mean(xv * xv, axis=-1, keepdims=True)
        return xv * lax.rsqrt(ms + EPS) * g

    at_start = (i % tiles_per_seq) == 0
    at_end = (i % tiles_per_seq) == tiles_per_seq - 1
    hn_ref[0:halo, :] = jnp.where(at_start, 0.0, norm(xp_ref[...])).astype(BF16)
    hn_ref[halo:halo + tm, :] = norm(xm_ref[...]).astype(BF16)
    hn_ref[halo + tm:, :] = jnp.where(at_end, 0.0, norm(xn_ref[...])).astype(BF16)

    def conv_cols(col0, buf):
        cs = slice(col0, col0 + tn)
        acc = jnp.dot(hn_ref[...], w_ref[:, cs].astype(BF16), preferred_element_type=F32)
        stage = stage_ref.at[buf]
        _slab_store_block(stage, 0, ext, acc)
        up, mid, dn = (_slab_rows(stage, halo + shift, tm) for shift in (-1, 0, 1))
        w = cw_ref[:, cs]
        return w[0:1] * up + w[1:2] * mid + w[2:3] * dn + cb_ref[:, cs]

    for k in range(hw // tn):
        o_ref[:, k * tn:(k + 1) * tn] = conv_cols(k * tn, 0).astype(o_ref.dtype)
    for k in range(hw // tn):
        u = conv_cols(hw + k * tn, 0) * conv_cols(2 * hw + k * tn, 1)
        o_ref[:, hw + k * tn:hw + (k + 1) * tn] = u.astype(o_ref.dtype)
    for k in range((w_ref.shape[1] - 3 * hw) // tn):
        cs = slice(3 * hw + k * tn, 3 * hw + (k + 1) * tn)
        acc = jnp.dot(hn_ref[halo:halo + tm, :], w_ref[:, cs].astype(BF16), preferred_element_type=F32)
        if k * tn < n_gelu_cols:
            out = 0.5 * acc * (1.0 + lax.erf(acc * math.sqrt(0.5)))
        else:
            out = 1.0 / (1.0 + jnp.exp(-acc))
        o_ref[:, 2 * hw + k * tn:2 * hw + (k + 1) * tn] = out.astype(o_ref.dtype)


def _in_projection(x2d, seq_len, g, w_in, conv_w, conv_b, hw, n_gelu_cols, tm, tn):
    t, d = x2d.shape
    n = w_in.shape[1]
    halo = ROW_HALO
    assert t % tm == 0 and seq_len % tm == 0 and tm % halo == 0 and tn % LANES == 0
    assert hw % tn == 0 and n_gelu_cols % tn == 0 and (n - 3 * hw) % tn == 0
    per = tm // halo
    last = t // halo - 1
    kern = functools.partial(_inproj_kernel, hw=hw, n_gelu_cols=n_gelu_cols, tn=tn,
                             tiles_per_seq=seq_len // tm)
    return pl.pallas_call(
        kern,
        grid=(t // tm,),
        in_specs=[
            pl.BlockSpec((tm, d), lambda i: (i, 0)),
            pl.BlockSpec((halo, d), lambda i: (jnp.maximum(i * per - 1, 0), 0)),
            pl.BlockSpec((halo, d), lambda i: (jnp.minimum((i + 1) * per, last), 0)),
            _const_spec((1, d)),
            _const_spec(w_in.shape),
            _const_spec(conv_w.shape),
            _const_spec((1, conv_b.shape[-1])),
        ],
        out_specs=pl.BlockSpec((tm, n - hw), lambda i: (i, 0)),
        out_shape=jax.ShapeDtypeStruct((t, n - hw), BF16),
        scratch_shapes=[pltpu.VMEM((tm + 2 * halo, d), BF16),
                        pltpu.VMEM((2, tn // LANES, tm + 2 * halo, LANES), F32)],
        compiler_params=_cparams(1),
        name="in_projection",
    )(x2d, x2d, x2d, g, w_in, conv_w, conv_b.reshape(1, -1))


def _filter_kernel(w1t_ref, w1c_ref, w1s_ref, b1_ref, w2_ref, b2_ref, w3_ref, b3_ref, fr_ref,
                   w4_ref, dec_ref, o_ref, *, seq_len, bands):
    tl = o_ref.shape[0]
    hi = lax.Precision.HIGHEST
    dot = functools.partial(jnp.dot, preferred_element_type=F32, precision=hi)
    r0 = pl.program_id(0) * tl
    pos = (r0 + lax.broadcasted_iota(jnp.int32, (1, tl), 1)).astype(F32)
    t = pos / float(seq_len - 1)
    band_step = (bands - 1 - 1e-4) / (bands - 1)
    band = 1e-4 + band_step * lax.broadcasted_iota(jnp.int32, (bands, 1), 0).astype(F32)
    phase = ((2.0 * math.pi / seq_len) * pos) * band
    a = fr_ref[...]
    z1 = w1t_ref[...] * t + dot(w1c_ref[...], jnp.cos(phase)) - dot(w1s_ref[...], jnp.sin(phase))
    h = jnp.sin(a * (z1 + b1_ref[...]))
    h = jnp.sin(a * (dot(w2_ref[...], h) + b2_ref[...]))
    h = jnp.sin(a * (dot(w3_ref[...], h) + b3_ref[...]))
    def split(v):
        v_hi = v.astype(BF16)
        return v_hi, (v - v_hi.astype(F32)).astype(BF16)

    h_hi, h_lo = split(h)
    w_hi, w_lo = split(w4_ref[...])
    lhs = jnp.concatenate([h_hi, h_lo, h_hi, h_lo], axis=0)
    rhs = jnp.concatenate([w_hi, w_hi, w_lo, w_lo], axis=0)
    f = lax.dot_general(lhs, rhs, (((0,), (0,)), ((), ())), preferred_element_type=F32)
    t_col = (r0 + lax.broadcasted_iota(jnp.int32, (tl, 1), 0)).astype(F32) / float(seq_len - 1)
    o_ref[...] = f * jnp.exp(-t_col * jnp.abs(dec_ref[...]))


def _implicit_filters(seq_len, w1, b1, w2, b2, w3, b3, freq, w4, decay, tl=1024):
    bands = (w1.shape[0] - 1) // 2
    n_out = w4.shape[1]
    assert seq_len % tl == 0
    col = lambda v: v.reshape(-1, 1)
    args = (w1[0:1].T, w1[1:1 + bands].T, w1[1 + bands:].T, col(b1), w2.T, col(b2), w3.T, col(b3),
            col(freq), w4, decay.reshape(1, -1))
    full = lambda a: pl.BlockSpec(a.shape, lambda i: (0, 0))
    kern = functools.partial(_filter_kernel, seq_len=seq_len, bands=bands)
    return pl.pallas_call(
        kern,
        grid=(seq_len // tl,),
        in_specs=[full(a) for a in args],
        out_specs=pl.BlockSpec((tl, n_out), lambda i: (i, 0)),
        out_shape=jax.ShapeDtypeStruct((seq_len, n_out), F32),
        compiler_params=_cparams(1),
        name="implicit_filter",
    )(*args)


class _FftPlan:
    def __init__(self, seq_len):
        n2 = FFT_INNER
        assert seq_len % n2 == 0
        self.seq_len = seq_len
        self.n = 2 * seq_len
        self.n2 = n2
        self.n1 = self.n // n2
        self.nb = seq_len // n2
        self.h1 = self.n1 // 2 + 1
        self.pa = 2 * self.h1
        self.nq = self.n1
        self.dump_block = self.n1
        assert self.nb % BF16_SUBLANE_TILE == 0
        n1, nb, h1, n = self.n1, self.nb, self.h1, self.n
        k1 = np.arange(h1)[:, None]
        ph = 2.0 * np.pi * k1 * np.arange(nb)[None, :] / n1
        w1 = np.zeros((self.pa, nb))
        w1[:h1] = np.cos(ph)
        w1[h1:2 * h1] = -np.sin(ph)
        k2 = np.arange(n2)[:, None]
        m2 = np.arange(n2)[None, :]
        w3 = np.zeros((h1, 2 * n2, 2 * n2))
        for k in range(h1):
            th = 2.0 * np.pi * (((k + n1 * k2) * m2) % n) / n
            er, ei = np.cos(th), -np.sin(th)
            w3[k] = np.block([[er, -ei], [ei, er]])
        ck = np.full(h1, 2.0)
        ck[0] = 1.0
        ck[-1] = 1.0
        phi = 2.0 * np.pi * np.arange(nb)[:, None] * np.arange(h1)[None, :] / n1
        mi = np.zeros((nb, self.nq))
        mi[:, :h1] = ck * np.cos(phi) / n
        mi[:, h1:] = (-ck * np.sin(phi) / n)[:, 1:h1 - 1]
        g = F32_SUBLANE_TILE
        w1k = np.einsum("kn,ij->kinj", w1, np.eye(g)).reshape(self.pa * g, nb * g)
        self.w1 = jnp.asarray(w1k, F32).astype(BF16)
        self.w3 = jnp.asarray(w3, F32).astype(BF16)
        self.w3t = jnp.asarray(np.transpose(w3, (0, 2, 1)), F32).astype(BF16)
        mik = np.einsum("nq,ij->niqj", mi, np.eye(g)).reshape(nb * g, self.nq * g)
        self.mi = jnp.asarray(mik, F32).astype(BF16)


def _slab_store_block(ref, start, size, val):
    for s in range(ref.shape[0]):
        ref[s, pl.ds(start, size), :] = val[:, s * LANES:(s + 1) * LANES]


def _slab_rows(ref, start, size):
    return jnp.concatenate([ref[s, pl.ds(start, size), :] for s in range(ref.shape[0])], axis=1)


def _fft_stage1(plan, load_rows, a_ref, w1_ref):
    w1 = w1_ref[...]
    g = F32_SUBLANE_TILE

    def body(c, carry):
        tiles = [load_rows(pl.multiple_of(blk * plan.n2 + c * g, g), g) for blk in range(plan.nb)]
        rhs = jnp.concatenate(tiles, axis=0).astype(BF16)
        a = jnp.dot(w1, rhs, preferred_element_type=F32)
        for k in range(plan.pa):
            _slab_store_block(a_ref, pl.multiple_of(k * plan.n2 + c * g, g), g, a[k * g:(k + 1) * g])
        return carry

    lax.fori_loop(0, plan.n2 // g, body, 0, unroll=True)


def _fft_stage2(plan, a_ref, w3_ref, k1):
    n2 = plan.n2
    aligned = (lambda r: r) if isinstance(k1, int) else (lambda r: pl.multiple_of(r, n2))
    re = _slab_rows(a_ref, aligned(k1 * n2), n2)
    im = _slab_rows(a_ref, aligned((plan.h1 + k1) * n2), n2)
    rhs = jnp.concatenate([re, im], axis=0).astype(BF16)
    return jnp.dot(w3_ref[k1], rhs, preferred_element_type=F32)


def _spectrum_kernel(plan, ff_ref, fb_ref, w1_ref, w3_ref, o_ref, af_ref, ab_ref):
    n2 = plan.n2
    _fft_stage1(plan, lambda start, size: ff_ref[pl.ds(start, size), :], af_ref, w1_ref)
    _fft_stage1(plan, lambda start, size: fb_ref[pl.ds(start, size), :], ab_ref, w1_ref)
    hb0 = fb_ref[0:1, :]

    def body(k1, carry):
        xf = _fft_stage2(plan, af_ref, w3_ref, k1)
        xb = _fft_stage2(plan, ab_ref, w3_ref, k1)
        o_ref[k1, 0:n2, :] = (xf[:n2] + xb[:n2] - hb0).astype(o_ref.dtype)
        o_ref[k1, n2:2 * n2, :] = (xf[n2:] - xb[n2:]).astype(o_ref.dtype)
        return carry

    lax.fori_loop(0, plan.h1, body, 0, unroll=True)


def _filter_spectrum(plan, filt, ct):
    seq_len, c2 = filt.shape
    c = c2 // 2
    assert c % ct == 0 and ct % LANES == 0
    nct = c // ct
    nslab = ct // LANES
    n2, h1, pa = plan.n2, plan.h1, plan.pa
    return pl.pallas_call(
        functools.partial(_spectrum_kernel, plan),
        grid=(nct,),
        in_specs=[
            pl.BlockSpec((seq_len, ct), lambda j: (0, j)),
            pl.BlockSpec((seq_len, ct), lambda j: (0, nct + j)),
            _const_spec(plan.w1.shape),
            _const_spec(plan.w3.shape),
        ],
        out_specs=pl.BlockSpec((h1, 2 * n2, ct), lambda j: (0, 0, j)),
        out_shape=jax.ShapeDtypeStruct((h1, 2 * n2, c), BF16),
        scratch_shapes=[
            pltpu.VMEM((nslab, n2 * pa, LANES), F32),
            pltpu.VMEM((nslab, n2 * pa, LANES), F32),
        ],
        compiler_params=_cparams(1),
        name="filter_spectrum",
    )(filt, filt, plan.w1, plan.w3)


def _hyena_kernel(plan, x0_ref, uin_ref, skip_ref, ks_ref, w1_ref, w3_ref, w3t_ref, mi_ref, o_ref,
                  u_ref, a_ref, b_ref):
    seq_len, n2, nb, h1 = plan.seq_len, plan.n2, plan.nb, plan.h1
    rows = COPY_ROWS
    g = F32_SUBLANE_TILE

    def load_u(c, carry):
        r0 = pl.multiple_of(c * rows, rows)
        _slab_store_block(u_ref, r0, rows, uin_ref[pl.ds(r0, rows), :].astype(F32))
        return carry

    lax.fori_loop(0, seq_len // rows, load_u, 0)

    _fft_stage1(plan, functools.partial(_slab_rows, u_ref), a_ref, w1_ref)

    for k1 in range(h1):
        x = _fft_stage2(plan, a_ref, w3_ref, k1)
        k = ks_ref[k1].astype(F32)
        xr, xi, kr, ki = x[:n2], x[n2:], k[:n2], k[n2:]
        y = jnp.concatenate([xr * kr - xi * ki, xr * ki + xi * kr], axis=0).astype(BF16)
        bm = jnp.dot(w3t_ref[k1], y, preferred_element_type=F32)
        _slab_store_block(b_ref, k1 * n2, n2, bm[:n2])
        im_block = plan.dump_block if k1 in (0, h1 - 1) else h1 - 1 + k1
        _slab_store_block(b_ref, im_block * n2, n2, bm[n2:])

    mi = mi_ref[...]
    skip = skip_ref[...]

    def last_stage(c, carry):
        tiles = [_slab_rows(b_ref, pl.multiple_of(q * n2 + c * g, g), g) for q in range(plan.nq)]
        rhs = jnp.concatenate(tiles, axis=0).astype(BF16)
        y = jnp.dot(mi, rhs, preferred_element_type=F32)
        for blk in range(nb):
            start = pl.multiple_of(blk * n2 + c * g, g)
            ucur = _slab_rows(u_ref, start, g)
            _slab_store_block(u_ref, start, g, y[blk * g:(blk + 1) * g] + skip * ucur)
        return carry

    lax.fori_loop(0, n2 // g, last_stage, 0, unroll=True)

    def gate_out(c, carry):
        r0 = pl.multiple_of(c * rows, rows)
        x0c = x0_ref[pl.ds(r0, rows), :].astype(F32)
        o_ref[pl.ds(r0, rows), :] = (x0c * _slab_rows(u_ref, r0, rows)).astype(o_ref.dtype)
        return carry

    lax.fori_loop(0, seq_len // rows, gate_out, 0)


def _hyena_operator(plan, proj3, skip, kspec, ct):
    bsz, seq_len, _ = proj3.shape
    c = skip.shape[-1]
    assert c % ct == 0 and ct % LANES == 0 and seq_len % COPY_ROWS == 0
    nct = c // ct
    nslab = ct // LANES
    n2, h1, pa = plan.n2, plan.h1, plan.pa
    part = lambda k: pl.BlockSpec((None, seq_len, ct), lambda j, b, k=k: (b, 0, k * nct + j))
    return pl.pallas_call(
        functools.partial(_hyena_kernel, plan),
        grid=(nct, bsz),
        in_specs=[
            part(0), part(1),
            pl.BlockSpec((1, ct), lambda j, b: (0, j)),
            pl.BlockSpec((h1, 2 * n2, ct), lambda j, b: (0, 0, j)),
            _const_spec(plan.w1.shape),
            _const_spec(plan.w3.shape),
            _const_spec(plan.w3t.shape),
            _const_spec(plan.mi.shape),
        ],
        out_specs=pl.BlockSpec((None, seq_len, ct), lambda j, b: (b, 0, j)),
        out_shape=jax.ShapeDtypeStruct((bsz, seq_len, c), BF16),
        scratch_shapes=[
            pltpu.VMEM((nslab, seq_len, LANES), F32),
            pltpu.VMEM((nslab, n2 * pa, LANES), F32),
            pltpu.VMEM((nslab, (plan.nq + 1) * n2, LANES), F32),
        ],
        compiler_params=_cparams(2),
        name="hyena_operator",
    )(proj3, proj3, skip.reshape(1, -1), kspec, plan.w1, plan.w3, plan.w3t, plan.mi)


def _merge_kernel(us_ref, vs_ref, ga_ref, gb_ref, ya_ref, x_ref, sg_ref, sw_ref, sbt_ref,
                  pa_ref, pb_ref, wo_ref, o_ref, yb_ref):
    heads, chunk, _ = sw_ref.shape
    tm, width = vs_ref.shape
    hd = width // heads
    ts = MERGE_SUB_ROWS
    for t0 in range(0, tm, ts):
        rows = slice(t0, t0 + ts)
        v = vs_ref[rows, :].astype(F32)
        ms = jnp.mean(v * v, axis=-1, keepdims=True)
        vn_all = (v * lax.rsqrt(ms + EPS) * sg_ref[...]).astype(BF16)
        for r in range(ts // chunk):
            rs = slice(t0 + r * chunk, t0 + (r + 1) * chunk)
            for g in range(heads):
                cs = slice(g * hd, (g + 1) * hd)
                s = (jnp.dot(sw_ref[g].astype(BF16), vn_all[r * chunk:(r + 1) * chunk, cs],
                             preferred_element_type=F32) + sbt_ref[:, g:g + 1])
                yb_ref[rs, cs] = (us_ref[rs, cs].astype(F32) * s).astype(BF16)
        pa = jnp.dot(ya_ref[rows, :], pa_ref[...].astype(BF16), preferred_element_type=F32)
        pb = jnp.dot(yb_ref[rows, :], pb_ref[...].astype(BF16), preferred_element_type=F32)
        merged = ga_ref[rows, :].astype(F32) * pa + gb_ref[rows, :].astype(F32) * pb
        o_ref[rows, :] = x_ref[rows, :] + jnp.dot(merged.astype(BF16), wo_ref[...].astype(BF16),
                                                  preferred_element_type=F32)


def _merge(proj, ya, x2d, sgu_g, sgu_w, sgu_b, pa_w, pb_w, wo_w, hy_cols, tm):
    t, d = x2d.shape
    width = sgu_g.shape[-1]
    heads, chunk, _ = sgu_w.shape
    assert t % tm == 0 and tm % MERGE_SUB_ROWS == 0 and MERGE_SUB_ROWS % chunk == 0
    assert hy_cols % width == 0 and ya.shape[1] == width
    base = hy_cols // width
    col = lambda k: pl.BlockSpec((tm, width), lambda i, k=k: (i, base + k))
    sbt = sgu_b.T
    return pl.pallas_call(
        _merge_kernel,
        grid=(t // tm,),
        in_specs=[
            col(0), col(1), col(2), col(3),
            pl.BlockSpec((tm, width), lambda i: (i, 0)),
            pl.BlockSpec((tm, d), lambda i: (i, 0)),
            _const_spec((1, width)),
            _const_spec(sgu_w.shape),
            _const_spec(sbt.shape),
            _const_spec(pa_w.shape),
            _const_spec(pb_w.shape),
            _const_spec(wo_w.shape),
        ],
        out_specs=pl.BlockSpec((tm, d), lambda i: (i, 0)),
        out_shape=jax.ShapeDtypeStruct((t, d), F32),
        scratch_shapes=[pltpu.VMEM((tm, width), BF16)],
        compiler_params=_cparams(1),
        name="sgu_merge",
    )(proj, proj, proj, proj, ya, x2d, sgu_g.reshape(1, -1), sgu_w, sbt, pa_w, pb_w, wo_w)


FFN_CHUNK = 256
FFN_SUB_ROWS = 512


def _ffn_kernel(xm_ref, xp_ref, xn_ref, g_ref, wup_ref, cw_ref, cb_ref, wdn_ref, fg_ref, o_ref,
                hs_ref, act_ref, stage_ref, *, tiles_per_seq, final_norm):
    tm, _ = xm_ref.shape
    hidden = wdn_ref.shape[0]
    halo = ROW_HALO
    i = pl.program_id(0)
    g = g_ref[...]

    def norm(xv):
        ms = jnp.mean(xv * xv, axis=-1, keepdims=True)
        return xv * lax.rsqrt(ms + EPS) * g

    at_start = (i % tiles_per_seq) == 0
    at_end = (i % tiles_per_seq) == tiles_per_seq - 1
    hs_ref[0:halo, :] = jnp.where(at_start, 0.0, norm(xp_ref[...])).astype(BF16)
    hs_ref[halo + tm:, :] = jnp.where(at_end, 0.0, norm(xn_ref[...])).astype(BF16)

    ts = FFN_SUB_ROWS
    ext = ts + 2 * halo
    for s in range(tm // ts):
        rs = slice(s * ts, (s + 1) * ts)
        hs_ref[halo + s * ts:halo + (s + 1) * ts, :] = norm(xm_ref[rs, :]).astype(BF16)
    for s in range(tm // ts):
        rs = slice(s * ts, (s + 1) * ts)
        h_ext = hs_ref[s * ts:s * ts + ext, :]
        h_main = hs_ref[halo + s * ts:halo + (s + 1) * ts, :]
        for k in range(hidden // FFN_CHUNK):
            cs = slice(k * FFN_CHUNK, (k + 1) * FFN_CHUNK)
            gs = slice(hidden + k * FFN_CHUNK, hidden + (k + 1) * FFN_CHUNK)
            a_ext = jnp.dot(h_ext, wup_ref[:, cs].astype(BF16), preferred_element_type=F32)
            gate = jnp.dot(h_main, wup_ref[:, gs].astype(BF16), preferred_element_type=F32)
            stage = stage_ref.at[k % 2]
            _slab_store_block(stage, 0, ext, a_ext)
            up, mid, dn = (_slab_rows(stage, halo + shift, ts) for shift in (-1, 0, 1))
            w = cw_ref[:, cs]
            c = w[0:1] * up + w[1:2] * mid + w[2:3] * dn + cb_ref[:, cs]
            act_ref[rs, cs] = (c / (1.0 + jnp.exp(-c)) * gate).astype(BF16)
        y = xm_ref[rs, :] + jnp.dot(act_ref[rs, :], wdn_ref[...].astype(BF16),
                                    preferred_element_type=F32)
        if final_norm:
            ms = jnp.mean(y * y, axis=-1, keepdims=True)
            y = y * lax.rsqrt(ms + EPS) * fg_ref[...]
        o_ref[rs, :] = y


def _ffn(x2d, seq_len, g, wup, conv_w, conv_b, wdn, final_g, final_norm, tm):
    t, d = x2d.shape
    hidden = wdn.shape[0]
    halo = ROW_HALO
    assert t % tm == 0 and seq_len % tm == 0 and tm % FFN_SUB_ROWS == 0 and FFN_SUB_ROWS % halo == 0
    assert hidden % FFN_CHUNK == 0
    per = tm // halo
    last = t // halo - 1
    kern = functools.partial(_ffn_kernel, tiles_per_seq=seq_len // tm, final_norm=final_norm)
    return pl.pallas_call(
        kern,
        grid=(t // tm,),
        in_specs=[
            pl.BlockSpec((tm, d), lambda i: (i, 0)),
            pl.BlockSpec((halo, d), lambda i: (jnp.maximum(i * per - 1, 0), 0)),
            pl.BlockSpec((halo, d), lambda i: (jnp.minimum((i + 1) * per, last), 0)),
            _const_spec((1, d)),
            _const_spec(wup.shape),
            _const_spec(conv_w.shape),
            _const_spec((1, hidden)),
            _const_spec(wdn.shape),
            _const_spec((1, d)),
        ],
        out_specs=pl.BlockSpec((tm, d), lambda i: (i, 0)),
        out_shape=jax.ShapeDtypeStruct((t, d), F32),
        scratch_shapes=[pltpu.VMEM((tm + 2 * halo, d), BF16), pltpu.VMEM((tm, hidden), BF16),
                        pltpu.VMEM((2, FFN_CHUNK // LANES, FFN_SUB_ROWS + 2 * halo, LANES), F32)],
        compiler_params=_cparams(1),
        name="ffn",
    )(x2d, x2d, x2d, g.reshape(1, -1), wup, conv_w, conv_b.reshape(1, -1), wdn,
      final_g.reshape(1, -1))


def kernel(x, norm1_g, w_in, hy_conv_w, hy_conv_b, filt_w1, filt_b1, filt_w2, filt_b2, filt_w3, filt_b3, filt_freq, filt_w4, hy_decay, hy_skip, sgu_norm_g, sgu_w, sgu_b, w_proj_hyena, w_proj_sgu, w_out, norm2_g, w_up, ffn_conv_w, ffn_conv_b, w_down, final_g):
    bsz, seq_len, d = x.shape
    depth = norm1_g.shape[0]
    hw = hy_skip.shape[-1]
    sw = sgu_norm_g.shape[-1]
    plan = _FftPlan(seq_len)
    x2d = x.reshape(bsz * seq_len, d)
    for i in range(depth):
        proj = _in_projection(x2d, seq_len, norm1_g[i].reshape(1, -1), w_in[i], hy_conv_w[i],
                              hy_conv_b[i], hw=hw, n_gelu_cols=2 * sw, tm=512, tn=512)
        filt = _implicit_filters(seq_len, filt_w1[i], filt_b1[i], filt_w2[i], filt_b2[i], filt_w3[i],
                                 filt_b3[i], filt_freq[i], filt_w4[i], hy_decay[i])
        kspec = _filter_spectrum(plan, filt, HYENA_CH)
        ya = _hyena_operator(plan, proj.reshape(bsz, seq_len, -1), hy_skip[i], kspec, HYENA_CH)
        x2d = _merge(proj, ya.reshape(bsz * seq_len, hw), x2d, sgu_norm_g[i], sgu_w[i],
                     sgu_b[i], w_proj_hyena[i], w_proj_sgu[i], w_out[i], hy_cols=2 * hw, tm=1024)
        x2d = _ffn(x2d, seq_len, norm2_g[i], w_up[i], ffn_conv_w[i], ffn_conv_b[i],
                   w_down[i], final_g, final_norm=(i == depth - 1), tm=1024)
    return x2d.reshape(bsz, seq_len, d)
```
